```python
import math
import jax, jax.numpy as jnp
from jax import lax
import numpy as np

D_MODEL = 2048
BATCH = 16
SEQ = 2048
DEPTH = 2

HEAD_DIM = 128
N_HEADS = D_MODEL // HEAD_DIM
MIX_WIDTH = N_HEADS * HEAD_DIM
A_HEADS = N_HEADS // 2
A_KV_HEADS = max(1, A_HEADS // 4)
A_GROUP = A_HEADS // A_KV_HEADS
A_Q = A_HEADS * HEAD_DIM
A_KV = A_KV_HEADS * HEAD_DIM
B_HEADS = N_HEADS - A_HEADS
B_QK_DIM = HEAD_DIM // 2
B_V_DIM = HEAD_DIM
B_QK2 = B_HEADS * 2 * B_QK_DIM
B_V = B_HEADS * B_V_DIM
GATE_EVEN = A_Q + B_V
IN_EVEN = A_Q + 2 * A_KV + 2 * B_QK2 + B_V + GATE_EVEN
C_HEADS = N_HEADS
C_WIDTH = C_HEADS * HEAD_DIM
C_PATTERNS = ((128, 1), (512, 4), (2048, 16))
IN_ODD = 4 * C_WIDTH
GRID_W = 64
ROPE_THETA = 10000.0
ROPE_PAIRS = HEAD_DIM // 4
Q_BLOCK = 128
NORM_EPS = 1e-6
SUBLN_EPS = 1e-5
NEG_INF = -1e30

kernel_name = 'hybrid_gqa_diff_dilated_encoder'


def rms_norm(x, g, eps=NORM_EPS):
    xf = x.astype(jnp.float32)
    y = xf * lax.rsqrt(jnp.mean(xf * xf, axis=-1, keepdims=True) + eps)
    return (y * g.astype(jnp.float32)).astype(x.dtype)


def alibi_slopes(n):
    return jnp.exp2(-8.0 * jnp.arange(1, n + 1, dtype=jnp.float32) / n)


def axial_rope_tables(s):
    rows = s // GRID_W
    row_ids = jnp.broadcast_to(jnp.arange(rows)[:, None], (rows, GRID_W)).reshape(s).astype(jnp.float32)
    col_ids = jnp.broadcast_to(jnp.arange(GRID_W)[None, :], (rows, GRID_W)).reshape(s).astype(jnp.float32)
    inv_freq = ROPE_THETA ** (-jnp.arange(ROPE_PAIRS, dtype=jnp.float32) / ROPE_PAIRS)
    ang_r = row_ids[:, None] * inv_freq[None, :]
    ang_c = col_ids[:, None] * inv_freq[None, :]
    return jnp.cos(ang_r), jnp.sin(ang_r), jnp.cos(ang_c), jnp.sin(ang_c)


def _rope_section(xs, cos, sin):
    x1, x2 = xs[..., :ROPE_PAIRS], xs[..., ROPE_PAIRS:]
    c = cos[None, :, None, :]
    sn = sin[None, :, None, :]
    return jnp.concatenate([x1 * c - x2 * sn, x2 * c + x1 * sn], axis=-1)


def apply_axial_rope(x, tables):
    cr, sr, cc, sc = tables
    xf = x.astype(jnp.float32)
    half = HEAD_DIM // 2
    out = jnp.concatenate([_rope_section(xf[..., :half], cr, sr),
                           _rope_section(xf[..., half:], cc, sc)], axis=-1)
    return out.astype(x.dtype)


def gqa_attention(q, k, v):
    b, s, _, d = q.shape
    nb = s // Q_BLOCK
    scale = 1.0 / math.sqrt(d)
    qb = q.reshape(b, nb, Q_BLOCK, A_KV_HEADS, A_GROUP, d).transpose(1, 0, 2, 3, 4, 5)

    def block(qi):
        sc = jnp.einsum('bqkgd,bskd->bkgqs', qi, k).astype(jnp.float32) * scale
        p = jax.nn.softmax(sc, axis=-1)
        return jnp.einsum('bkgqs,bskd->bqkgd', p.astype(v.dtype), v)

    o = lax.map(block, qb)
    return o.transpose(1, 0, 2, 3, 4, 5).reshape(b, s, A_HEADS * d)


def diff_attention(q1, q2, k1, k2, v, lam, slopes):
    b, s, h, dq = q1.shape
    nb = s // Q_BLOCK
    scale = 1.0 / math.sqrt(dq)
    pos = jnp.arange(s)
    q1b = q1.reshape(b, nb, Q_BLOCK, h, dq).transpose(1, 0, 2, 3, 4)
    q2b = q2.reshape(b, nb, Q_BLOCK, h, dq).transpose(1, 0, 2, 3, 4)
    starts = jnp.arange(nb) * Q_BLOCK

    def block(args):
        q1i, q2i, t0 = args
        tq = t0 + jnp.arange(Q_BLOCK)
        dist = jnp.abs(tq[:, None] - pos[None, :]).astype(jnp.float32)
        bias = -slopes[:, None, None] * dist[None]
        s1 = jnp.einsum('bqhd,bshd->bhqs', q1i, k1).astype(jnp.float32) * scale + bias
        s2 = jnp.einsum('bqhd,bshd->bhqs', q2i, k2).astype(jnp.float32) * scale + bias
        p = jax.nn.softmax(s1, axis=-1) - lam * jax.nn.softmax(s2, axis=-1)
        return jnp.einsum('bhqs,bshd->bqhd', p.astype(v.dtype), v)

    o = lax.map(block, (q1b, q2b, starts))
    return o.transpose(1, 0, 2, 3, 4).reshape(b, s, h, v.shape[-1])


def dilated_window_attention(q, k, v, r, half, slopes):
    b, s, h, d = q.shape
    L = s // r
    nb = -(-L // half)
    lp = nb * half
    scale = 1.0 / math.sqrt(d)

    def sub(t):
        return t.reshape(b, L, r, h, d)

    qs = jnp.pad(sub(q), ((0, 0), (0, lp - L), (0, 0), (0, 0), (0, 0))).reshape(b, nb, half, r, h, d)
    pad_k = ((0, 0), (half, lp - L + half), (0, 0), (0, 0), (0, 0))
    kp = jnp.pad(sub(k), pad_k)
    vp = jnp.pad(sub(v), pad_k)
    win = jnp.arange(nb)[:, None] * half + jnp.arange(3 * half)[None, :]
    kb = jnp.take(kp, win, axis=1)
    vb = jnp.take(vp, win, axis=1)
    mq = jnp.arange(nb)[:, None] * half + jnp.arange(half)[None, :]
    mk = win - half
    rel = mk[:, None, :] - mq[:, :, None]
    valid = (jnp.abs(rel) <= half) & (mk[:, None, :] >= 0) & (mk[:, None, :] < L)
    dist = (jnp.abs(rel) * r).astype(jnp.float32)
    bias = -slopes[None, :, None, None] * dist[:, None]
    sc = jnp.einsum('bnqchd,bnkchd->bnchqk', qs, kb).astype(jnp.float32) * scale + bias[None, :, None]
    sc = jnp.where(valid[None, :, None, None], sc, NEG_INF)
    m = jnp.max(sc, axis=-1, keepdims=True)
    e = jnp.exp(sc - m)
    den = jnp.sum(e, axis=-1, keepdims=True)
    o = jnp.einsum('bnchqk,bnkchd->bnqchd', (e / den).astype(v.dtype), vb)
    lse = (m + jnp.log(den))[..., 0]
    o = o.reshape(b, lp, r, h, d)[:, :L].reshape(b, s, h, d)
    lse = lse.transpose(0, 1, 4, 2, 3).reshape(b, lp, r, h)[:, :L].reshape(b, s, h)
    return o, lse


def setup_inputs(seed: int = 0) -> dict:
    key = jax.random.key(seed)
    ks = jax.random.split(key, 16)
    n_even = (DEPTH + 1) // 2
    n_odd = DEPTH // 2
    f32 = jnp.float32

    def gain(k, shape):
        return 1.0 + 0.02 * jax.random.normal(k, shape, f32)

    return {
        'x': jax.random.normal(ks[0], (BATCH, SEQ, D_MODEL), f32),
        'ln_even_g': gain(ks[1], (n_even, D_MODEL)),
        'w_in_even': jax.random.normal(ks[2], (n_even, D_MODEL, IN_EVEN), f32) * D_MODEL ** -0.5,
        'a_q_norm_g': gain(ks[3], (n_even, HEAD_DIM)),
        'a_k_norm_g': gain(ks[4], (n_even, HEAD_DIM)),
        'b_lambda_q1': 0.1 * jax.random.normal(ks[5], (n_even, B_QK_DIM), f32),
        'b_lambda_k1': 0.1 * jax.random.normal(ks[6], (n_even, B_QK_DIM), f32),
        'b_lambda_q2': 0.1 * jax.random.normal(ks[7], (n_even, B_QK_DIM), f32),
        'b_lambda_k2': 0.1 * jax.random.normal(ks[8], (n_even, B_QK_DIM), f32),
        'b_subln_g': gain(ks[9], (n_even, B_V_DIM)),
        'w_out_even': jax.random.normal(ks[10], (n_even, MIX_WIDTH, D_MODEL), f32) * MIX_WIDTH ** -0.5,
        'ln_odd_g': gain(ks[11], (n_odd, D_MODEL)),
        'w_in_odd': jax.random.normal(ks[12], (n_odd, D_MODEL, IN_ODD), f32) * D_MODEL ** -0.5,
        'w_out_odd': jax.random.normal(ks[13], (n_odd, C_WIDTH, D_MODEL), f32) * C_WIDTH ** -0.5,
        'final_norm_g': gain(ks[14], (D_MODEL,)),
    }


def reference(x, ln_even_g, w_in_even, a_q_norm_g, a_k_norm_g, b_lambda_q1, b_lambda_k1,
              b_lambda_q2, b_lambda_k2, b_subln_g, w_out_even, ln_odd_g, w_in_odd, w_out_odd,
              final_norm_g):
    b, s, _ = x.shape
    rope_tables = axial_rope_tables(s)
    slopes_b = alibi_slopes(B_HEADS)
    slopes_c = alibi_slopes(C_HEADS)
    even_splits = np.cumsum([A_Q, A_KV, A_KV, B_QK2, B_QK2, B_V]).tolist()
    odd_splits = [C_WIDTH, 2 * C_WIDTH, 3 * C_WIDTH]
    h = x
    for layer in range(DEPTH):
        i = layer // 2
        if layer % 2 == 0:
            u = rms_norm(h, ln_even_g[i])
            proj = jnp.einsum('bsd,de->bse', u, w_in_even[i])
            qa, ka, va, qb, kb, vb, gate = jnp.split(proj, even_splits, axis=-1)
            qa = apply_axial_rope(rms_norm(qa.reshape(b, s, A_HEADS, HEAD_DIM), a_q_norm_g[i]), rope_tables)
            ka = apply_axial_rope(rms_norm(ka.reshape(b, s, A_KV_HEADS, HEAD_DIM), a_k_norm_g[i]), rope_tables)
            va = va.reshape(b, s, A_KV_HEADS, HEAD_DIM)
            ya = gqa_attention(qa, ka, va)
            qb = qb.reshape(b, s, B_HEADS, 2, B_QK_DIM)
            kb = kb.reshape(b, s, B_HEADS, 2, B_QK_DIM)
            vb = vb.reshape(b, s, B_HEADS, B_V_DIM)
            lambda_init = 0.8 - 0.6 * math.exp(-0.3 * layer)
            lam = (jnp.exp(jnp.sum(b_lambda_q1[i].astype(jnp.float32) * b_lambda_k1[i].astype(jnp.float32)))
                   - jnp.exp(jnp.sum(b_lambda_q2[i].astype(jnp.float32) * b_lambda_k2[i].astype(jnp.float32)))
                   + lambda_init)
            yb = diff_attention(qb[:, :, :, 0], qb[:, :, :, 1], kb[:, :, :, 0], kb[:, :, :, 1], vb, lam, slopes_b)
            yb = (rms_norm(yb, b_subln_g[i], SUBLN_EPS) * (1.0 - lambda_init)).reshape(b, s, B_V)
            y = jnp.concatenate([ya, yb], axis=-1) * jax.nn.silu(gate)
            h = h + jnp.einsum('bse,ed->bsd', y, w_out_even[i])
        else:
            u = rms_norm(h, ln_odd_g[i])
            proj = jnp.einsum('bsd,de->bse', u, w_in_odd[i])
            qc, kc, vc, gate = jnp.split(proj, odd_splits, axis=-1)
            qc = qc.reshape(b, s, C_HEADS, HEAD_DIM)
            kc = kc.reshape(b, s, C_HEADS, HEAD_DIM)
            vc = vc.reshape(b, s, C_HEADS, HEAD_DIM)
            outs, lses = [], []
            for window, dil in C_PATTERNS:
                o, l = dilated_window_attention(qc, kc, vc, dil, window // (2 * dil), slopes_c)
                outs.append(o)
                lses.append(l)
            wts = jax.nn.softmax(jnp.stack(lses, axis=0), axis=0)
            yc = jnp.sum(wts[..., None].astype(vc.dtype) * jnp.stack(outs, axis=0), axis=0)
            y = yc.reshape(b, s, C_WIDTH) * jax.nn.silu(gate)
            h = h + jnp.einsum('bse,ed->bsd', y, w_out_odd[i])
    return rms_norm(h, final_norm_g)
```

```python
import functools
import math

import jax
import jax.numpy as jnp
from jax import lax
from jax.experimental import pallas as pl
from jax.experimental.pallas import tpu as pltpu

F32 = jnp.float32
BF16 = jnp.bfloat16

HEAD_DIM = 128
A_HEADS = 8
A_KV_HEADS = 2
A_GROUP = A_HEADS // A_KV_HEADS
B_HEADS = 8
B_QK_DIM = 64
C_HEADS = 16
C_PATTERNS = ((128, 1), (512, 4), (2048, 16))
GRID_W = 64
ROPE_THETA = 10000.0
ROPE_PAIRS = HEAD_DIM // 4
NORM_EPS = 1e-6
SUBLN_EPS = 1e-5
NEG_INF = -1e30

A_Q = A_HEADS * HEAD_DIM
A_KV = A_KV_HEADS * HEAD_DIM
B_W = B_HEADS * HEAD_DIM
MIX = A_Q + B_W
EV_GATE, EV_QA, EV_KA, EV_VA = 0, MIX, MIX + A_Q, MIX + A_Q + A_KV
EV_QB = EV_VA + A_KV
EV_KB = EV_QB + B_W
EV_VB = EV_KB + B_W
IN_EVEN = EV_VB + B_W
C_W = C_HEADS * HEAD_DIM
OD_GATE, OD_Q, OD_K, OD_V = 0, C_W, 2 * C_W, 3 * C_W
IN_ODD = 4 * C_W

V7X_VMEM_LIMIT_BYTES = 56 * 1024 * 1024


def _params(n_axes):
    return pltpu.CompilerParams(dimension_semantics=("arbitrary",) * n_axes,
                                vmem_limit_bytes=V7X_VMEM_LIMIT_BYTES)


def _rms(x, eps):
    return x * lax.rsqrt(jnp.mean(x * x, axis=-1, keepdims=True) + eps)


def _rms_inproj_kernel(x_ref, g_ref, w_ref, o_ref, u_scr, *, strip):
    @pl.when(pl.program_id(1) == 0)
    def _():
        def body(r, c):
            sl = pl.ds(pl.multiple_of(r * strip, strip), strip)
            u_scr[sl, :] = (_rms(x_ref[sl, :], NORM_EPS) * g_ref[...]).astype(BF16)
            return c
        lax.fori_loop(0, x_ref.shape[0] // strip, body, 0)

    o_ref[...] = jnp.dot(u_scr[...], w_ref[...], preferred_element_type=F32).astype(o_ref.dtype)


def _rms_inproj(x2d, g, w, *, tm, tn):
    m, d = x2d.shape
    n = w.shape[1]
    strip = min(128, tm)
    return pl.pallas_call(
        functools.partial(_rms_inproj_kernel, strip=strip),
        grid=(m // tm, n // tn),
        in_specs=[pl.BlockSpec((tm, d), lambda i, j: (i, 0)),
                  pl.BlockSpec((1, d), lambda i, j: (0, 0)),
                  pl.BlockSpec((d, tn), lambda i, j: (0, j))],
        out_specs=pl.BlockSpec((tm, tn), lambda i, j: (i, j)),
        out_shape=jax.ShapeDtypeStruct((m, n), BF16),
        scratch_shapes=[pltpu.VMEM((tm, d), BF16)],
        compiler_params=_params(2),
        name="rms_inproj",
    )(x2d, g, w)


def _matmul_kernel(a_ref, w_ref, o_ref):
    o_ref[...] = jnp.dot(a_ref[...], w_ref[...], preferred_element_type=F32).astype(o_ref.dtype)


def _matmul(a, w, *, tm, tn):
    m, d = a.shape
    n = w.shape[1]
    return pl.pallas_call(
        _matmul_kernel,
        grid=(m // tm, n // tn),
        in_specs=[pl.BlockSpec((tm, d), lambda i, j: (i, 0)),
                  pl.BlockSpec((d, tn), lambda i, j: (0, j))],
        out_specs=pl.BlockSpec((tm, tn), lambda i, j: (i, j)),
        out_shape=jax.ShapeDtypeStruct((m, n), BF16),
        compiler_params=_params(2),
        name="inproj",
    )(a, w)


def _gate_outproj_kernel(*refs, n_y, last):
    y_refs = refs[:n_y]
    gate_ref, w_ref, res_ref, g_ref = refs[n_y:n_y + 4]
    outs = refs[n_y + 4:]
    y = jnp.concatenate([r[...].astype(F32) for r in y_refs], axis=-1) if n_y > 1 else y_refs[0][...].astype(F32)
    gate = gate_ref[...].astype(F32)
    y = (y * (gate / (1.0 + jnp.exp(-gate)))).astype(BF16)
    h = jnp.dot(y, w_ref[...], preferred_element_type=F32) + res_ref[...]
    u = _rms(h, NORM_EPS) * g_ref[...]
    if last:
        outs[0][...] = u.astype(outs[0].dtype)
    else:
        outs[0][...] = h
        outs[1][...] = u.astype(outs[1].dtype)


def _gate_outproj(ys, proj, w, res, g_next, *, tm, last):
    m, d = res.shape
    k = w.shape[0]
    in_specs = [pl.BlockSpec((tm, y.shape[1]), lambda i: (i, 0)) for y in ys]
    in_specs += [pl.BlockSpec((tm, k), lambda i: (i, 0)),
                 pl.BlockSpec((k, d), lambda i: (0, 0)),
                 pl.BlockSpec((tm, d), lambda i: (i, 0)),
                 pl.BlockSpec((1, d), lambda i: (0, 0))]
    if last:
        out_shape = [jax.ShapeDtypeStruct((m, d), F32)]
    else:
        out_shape = [jax.ShapeDtypeStruct((m, d), F32), jax.ShapeDtypeStruct((m, d), BF16)]
    out_specs = [pl.BlockSpec((tm, d), lambda i: (i, 0)) for _ in out_shape]
    return pl.pallas_call(
        functools.partial(_gate_outproj_kernel, n_y=len(ys), last=last),
        grid=(m // tm,),
        in_specs=in_specs,
        out_specs=out_specs,
        out_shape=out_shape,
        compiler_params=_params(1),
        name="gate_outproj",
    )(*ys, proj, w, res, g_next)


def _softmax_pv(q_scr, k_ref, v_ref, s_scr, bias_fn):
    nk, m_rows, kc = s_scr.shape
    q = q_scr[...]
    m = jnp.full((m_rows, 1), -jnp.inf, F32)
    for c in range(nk):
        s = lax.dot_general(q, k_ref[c * kc:(c + 1) * kc, :], (((1,), (1,)), ((), ())),
                            preferred_element_type=F32)
        if bias_fn is not None:
            s = s + bias_fn(c)
        s_scr[c] = s
        m = jnp.maximum(m, jnp.max(s, axis=-1, keepdims=True))
    l = jnp.zeros((m_rows, 1), F32)
    acc = jnp.zeros((m_rows, HEAD_DIM), F32)
    for c in range(nk):
        e = jnp.exp(s_scr[c] - m)
        l = l + jnp.sum(e, axis=-1, keepdims=True)
        acc = acc + jnp.dot(e.astype(BF16), v_ref[c * kc:(c + 1) * kc, :], preferred_element_type=F32)
    return acc, l


def _norm_rope(x, g, cos, sin_up, sin_dn, scale):
    y = _rms(x.astype(F32), NORM_EPS) * g
    out = y * cos + pltpu.roll(y, 96, 1) * sin_up + pltpu.roll(y, 32, 1) * sin_dn
    if scale != 1.0:
        out = out * scale
    return out.astype(BF16)


def _rope_tables(s):
    pos = jnp.arange(s)
    row_ids = (pos // GRID_W).astype(F32)
    col_ids = (pos % GRID_W).astype(F32)
    inv_freq = ROPE_THETA ** (-jnp.arange(ROPE_PAIRS, dtype=F32) / ROPE_PAIRS)
    ang_r = row_ids[:, None] * inv_freq[None, :]
    ang_c = col_ids[:, None] * inv_freq[None, :]
    cr, sr, cc, sc = jnp.cos(ang_r), jnp.sin(ang_r), jnp.cos(ang_c), jnp.sin(ang_c)
    z = jnp.zeros_like(sr)
    cos = jnp.concatenate([cr, cr, cc, cc], axis=-1)
    sin_up = jnp.concatenate([-sr, z, -sc, z], axis=-1)
    sin_dn = jnp.concatenate([z, sr, z, sc], axis=-1)
    return cos, sin_up, sin_dn


def _slope(h, n_heads):
    hv = jnp.full((1, 1), h + 1, jnp.int32).astype(F32)
    return jnp.exp2(-8.0 * hv / n_heads)


def _rel_pos(qi, tq, c, kc):
    r = lax.broadcasted_iota(jnp.int32, (tq, kc), 0)
    k = lax.broadcasted_iota(jnp.int32, (tq, kc), 1)
    return (r - k) + (qi * tq - c * kc)


def _gqa_kernel(q_ref, k_ref, v_ref, gq_ref, gk_ref, cos_ref, sup_ref, sdn_ref, o_ref,
                k_scr, q_scr, s_scr, *, tq, strip):
    qi = pl.program_id(2)
    s_len = k_ref.shape[0]

    @pl.when(qi == 0)
    def _():
        def body(r, c):
            sl = pl.ds(pl.multiple_of(r * strip, strip), strip)
            k_scr[sl, :] = _norm_rope(k_ref[sl, :], gk_ref[...], cos_ref[sl, :], sup_ref[sl, :],
                                      sdn_ref[sl, :], 1.0)
            return c
        lax.fori_loop(0, s_len // strip, body, 0)

    rows = pl.ds(pl.multiple_of(qi * tq, tq), tq)
    cos, sup, sdn = cos_ref[rows, :], sup_ref[rows, :], sdn_ref[rows, :]
    scale = 1.0 / math.sqrt(HEAD_DIM)
    for g in range(A_GROUP):
        q_scr[g * tq:(g + 1) * tq, :] = _norm_rope(q_ref[:, g * HEAD_DIM:(g + 1) * HEAD_DIM], gq_ref[...],
                                                   cos, sup, sdn, scale)
    acc, l = _softmax_pv(q_scr, k_scr, v_ref, s_scr, None)
    o = acc / l
    for g in range(A_GROUP):
        o_ref[:, g * HEAD_DIM:(g + 1) * HEAD_DIM] = o[g * tq:(g + 1) * tq].astype(o_ref.dtype)


def _gqa_attention(proj, gq, gk, tables, *, b, s, tq, kc):
    nq = s // tq
    gw = A_GROUP * HEAD_DIM
    full = lambda bi, kv, qi: (0, 0)
    return pl.pallas_call(
        functools.partial(_gqa_kernel, tq=tq, strip=min(256, s)),
        grid=(b, A_KV_HEADS, nq),
        in_specs=[pl.BlockSpec((tq, gw), lambda bi, kv, qi: (bi * nq + qi, EV_QA // gw + kv)),
                  pl.BlockSpec((s, HEAD_DIM), lambda bi, kv, qi: (bi, EV_KA // HEAD_DIM + kv)),
                  pl.BlockSpec((s, HEAD_DIM), lambda bi, kv, qi: (bi, EV_VA // HEAD_DIM + kv)),
                  pl.BlockSpec((1, HEAD_DIM), full),
                  pl.BlockSpec((1, HEAD_DIM), full),
                  pl.BlockSpec((s, HEAD_DIM), full),
                  pl.BlockSpec((s, HEAD_DIM), full),
                  pl.BlockSpec((s, HEAD_DIM), full)],
        out_specs=pl.BlockSpec((tq, gw), lambda bi, kv, qi: (bi * nq + qi, kv)),
        out_shape=jax.ShapeDtypeStruct((b * s, A_Q), BF16),
        scratch_shapes=[pltpu.VMEM((s, HEAD_DIM), BF16),
                        pltpu.VMEM((A_GROUP * tq, HEAD_DIM), BF16),
                        pltpu.VMEM((s // kc, A_GROUP * tq, kc), F32)],
        compiler_params=_params(3),
        name="gqa_attention",
    )(proj, proj, proj, gq, gk, *tables)


def _diff_kernel(q_ref, k_ref, v_ref, lq1_ref, lk1_ref, lq2_ref, lk2_ref, g_ref, o_ref,
                 q_scr, s_scr, *, tq, lambda_init):
    h = pl.program_id(1)
    qi = pl.program_id(2)
    kc = s_scr.shape[2]
    lane = lax.broadcasted_iota(jnp.int32, (tq, HEAD_DIM), 1)
    qf = q_ref[...].astype(F32) * (1.0 / math.sqrt(B_QK_DIM))
    q_scr[0:tq, :] = jnp.where(lane < B_QK_DIM, qf, 0.0).astype(BF16)
    q_scr[tq:2 * tq, :] = jnp.where(lane < B_QK_DIM, 0.0, qf).astype(BF16)
    slope = _slope(h, B_HEADS)

    def bias(c):
        bb = -slope * jnp.abs(_rel_pos(qi, tq, c, kc).astype(F32))
        return jnp.concatenate([bb, bb], axis=0)

    acc, l = _softmax_pv(q_scr, k_ref, v_ref, s_scr, bias)
    lam = (jnp.exp(jnp.sum(lq1_ref[...] * lk1_ref[...], axis=-1, keepdims=True))
           - jnp.exp(jnp.sum(lq2_ref[...] * lk2_ref[...], axis=-1, keepdims=True)) + lambda_init)
    o = acc[0:tq] / l[0:tq] - lam * (acc[tq:2 * tq] / l[tq:2 * tq])
    o = _rms(o, SUBLN_EPS) * g_ref[...] * (1.0 - lambda_init)
    o_ref[...] = o.astype(o_ref.dtype)


def _diff_attention(proj, lq1, lk1, lq2, lk2, g, *, b, s, tq, kc, lambda_init):
    nq = s // tq
    full = lambda bi, h, qi: (0, 0)
    return pl.pallas_call(
        functools.partial(_diff_kernel, tq=tq, lambda_init=lambda_init),
        grid=(b, B_HEADS, nq),
        in_specs=[pl.BlockSpec((tq, HEAD_DIM), lambda bi, h, qi: (bi * nq + qi, EV_QB // HEAD_DIM + h)),
                  pl.BlockSpec((s, HEAD_DIM), lambda bi, h, qi: (bi, EV_KB // HEAD_DIM + h)),
                  pl.BlockSpec((s, HEAD_DIM), lambda bi, h, qi: (bi, EV_VB // HEAD_DIM + h)),
                  pl.BlockSpec((1, B_QK_DIM), full),
                  pl.BlockSpec((1, B_QK_DIM), full),
                  pl.BlockSpec((1, B_QK_DIM), full),
                  pl.BlockSpec((1, B_QK_DIM), full),
                  pl.BlockSpec((1, HEAD_DIM), full)],
        out_specs=pl.BlockSpec((tq, HEAD_DIM), lambda bi, h, qi: (bi * nq + qi, h)),
        out_shape=jax.ShapeDtypeStruct((b * s, B_W), BF16),
        scratch_shapes=[pltpu.VMEM((2 * tq, HEAD_DIM), BF16),
                        pltpu.VMEM((s // kc, 2 * tq, kc), F32)],
        compiler_params=_params(3),
        name="diff_attention",
    )(proj, proj, proj, lq1, lk1, lq2, lk2, g)


def _dilated_kernel(q_ref, k_ref, v_ref, o_ref, q_scr, s_scr, *, tq):
    h = pl.program_id(1)
    qi = pl.program_id(2)
    kc = s_scr.shape[2]
    q_scr[...] = (q_ref[...].astype(F32) * (1.0 / math.sqrt(HEAD_DIM))).astype(BF16)
    slope = _slope(h, C_HEADS)

    def bias(c):
        d = _rel_pos(qi, tq, c, kc)
        ad = jnp.abs(d.astype(F32))
        cnt = jnp.zeros((tq, kc), F32)
        for window, dil in C_PATTERNS:
            cnt = cnt + jnp.where(((d & (dil - 1)) == 0) & (ad <= float(window // 2)), 1.0, 0.0)
        log_cnt = jnp.where(cnt > 2.5, math.log(3.0), jnp.where(cnt > 1.5, math.log(2.0), 0.0))
        return jnp.where(cnt > 0.5, log_cnt - slope * ad, NEG_INF)

    acc, l = _softmax_pv(q_scr, k_ref, v_ref, s_scr, bias)
    o_ref[...] = (acc / l).astype(o_ref.dtype)


def _dilated_attention(proj, *, b, s, tq, kc):
    nq = s // tq
    return pl.pallas_call(
        functools.partial(_dilated_kernel, tq=tq),
        grid=(b, C_HEADS, nq),
        in_specs=[pl.BlockSpec((tq, HEAD_DIM), lambda bi, h, qi: (bi * nq + qi, OD_Q // HEAD_DIM + h)),
                  pl.BlockSpec((s, HEAD_DIM), lambda bi, h, qi: (bi, OD_K // HEAD_DIM + h)),
                  pl.BlockSpec((s, HEAD_DIM), lambda bi, h, qi: (bi, OD_V // HEAD_DIM + h))],
        out_specs=pl.BlockSpec((tq, HEAD_DIM), lambda bi, h, qi: (bi * nq + qi, h)),
        out_shape=jax.ShapeDtypeStruct((b * s, C_W), BF16),
        scratch_shapes=[pltpu.VMEM((tq, HEAD_DIM), BF16),
                        pltpu.VMEM((s // kc, tq, kc), F32)],
        compiler_params=_params(3),
        name="dilated_attention",
    )(proj, proj, proj)


def _largest_tile(n, cap):
    t = min(n, cap)
    while n % t:
        t //= 2
    return t


def kernel(x, ln_even_g, w_in_even, a_q_norm_g, a_k_norm_g, b_lambda_q1, b_lambda_k1, b_lambda_q2,
           b_lambda_k2, b_subln_g, w_out_even, ln_odd_g, w_in_odd, w_out_odd, final_norm_g):
    b, s, d = x.shape
    m = b * s
    depth = ln_even_g.shape[0] + ln_odd_g.shape[0]
    assert s % GRID_W == 0 and s % HEAD_DIM == 0 and d == MIX
    tq = _largest_tile(s, 256)
    kc = _largest_tile(s, 512)
    tables = _rope_tables(s)
    row = lambda v: v.reshape(1, -1).astype(F32)

    def norm_gain(layer):
        if layer == depth:
            return row(final_norm_g)
        return row(ln_even_g[layer // 2] if layer % 2 == 0 else ln_odd_g[layer // 2])

    h = x.reshape(m, d)
    u = None
    for layer in range(depth):
        i = layer // 2
        last = layer == depth - 1
        if layer % 2 == 0:
            w = w_in_even[i]
            w = jnp.concatenate([w[:, IN_EVEN - MIX:], w[:, :IN_EVEN - MIX]], axis=1).astype(BF16)
            if u is None:
                proj = _rms_inproj(h, norm_gain(layer), w, tm=_largest_tile(m, 1024), tn=IN_EVEN // 4)
            else:
                proj = _matmul(u, w, tm=_largest_tile(m, 1024), tn=IN_EVEN // 4)
            ya = _gqa_attention(proj, row(a_q_norm_g[i]), row(a_k_norm_g[i]), tables, b=b, s=s, tq=tq, kc=kc)
            lambda_init = 0.8 - 0.6 * math.exp(-0.3 * layer)
            yb = _diff_attention(proj, row(b_lambda_q1[i]), row(b_lambda_k1[i]), row(b_lambda_q2[i]),
                                 row(b_lambda_k2[i]), row(b_subln_g[i]), b=b, s=s, tq=tq, kc=kc,
                                 lambda_init=lambda_init)
            ys, w_out = [ya, yb], w_out_even[i]
        else:
            w = w_in_odd[i]
            w = jnp.concatenate([w[:, IN_ODD - C_W:], w[:, :IN_ODD - C_W]], axis=1).astype(BF16)
            if u is None:
                proj = _rms_inproj(h, norm_gain(layer), w, tm=_largest_tile(m, 1024), tn=IN_ODD // 4)
            else:
                proj = _matmul(u, w, tm=_largest_tile(m, 1024), tn=IN_ODD // 4)
            ys, w_out = [_dilated_attention(proj, b=b, s=s, tq=tq, kc=kc)], w_out_odd[i]
        outs = _gate_outproj(ys, proj, w_out.astype(BF16), h, norm_gain(layer + 1),
                             tm=_largest_tile(m, 256), last=last)
        if last:
            return outs[0].reshape(b, s, d)
        h, u = outs
```

```python
import functools
import math

import jax
import jax.numpy as jnp
from jax import lax
from jax.experimental import pallas as pl
from jax.experimental.pallas import tpu as pltpu

F32 = jnp.float32
BF16 = jnp.bfloat16

HEAD_DIM = 128
A_HEADS = 8
A_KV_HEADS = 2
A_GROUP = A_HEADS // A_KV_HEADS
B_HEADS = 8
B_QK_DIM = 64
C_HEADS = 16
C_PATTERNS = ((128, 1), (512, 4), (2048, 16))
GRID_W = 64
ROPE_THETA = 10000.0
ROPE_PAIRS = HEAD_DIM // 4
NORM_EPS = 1e-6
SUBLN_EPS = 1e-5
NEG_INF = -1e30
LOG2E = math.log2(math.e)
QT = 256
SCORE_SLOTS = 3

A_Q = A_HEADS * HEAD_DIM
A_KV = A_KV_HEADS * HEAD_DIM
B_W = B_HEADS * HEAD_DIM
MIX = A_Q + B_W
EV_GATE, EV_QA, EV_KA, EV_VA = 0, MIX, MIX + A_Q, MIX + A_Q + A_KV
EV_QB = EV_VA + A_KV
EV_KB = EV_QB + B_W
EV_VB = EV_KB + B_W
IN_EVEN = EV_VB + B_W
C_W = C_HEADS * HEAD_DIM
OD_GATE, OD_Q, OD_K, OD_V = 0, C_W, 2 * C_W, 3 * C_W
IN_ODD = 4 * C_W

V7X_VMEM_LIMIT_BYTES = 56 * 1024 * 1024


def _params(n_axes):
    return pltpu.CompilerParams(dimension_semantics=("arbitrary",) * n_axes,
                                vmem_limit_bytes=V7X_VMEM_LIMIT_BYTES)


def _rms(x, eps):
    return x * lax.rsqrt(jnp.mean(x * x, axis=-1, keepdims=True) + eps)


def _rms_inproj_kernel(x_ref, g_ref, w_ref, o_ref, u_scr, *, strip):
    @pl.when(pl.program_id(1) == 0)
    def _():
        def body(r, c):
            sl = pl.ds(pl.multiple_of(r * strip, strip), strip)
            u_scr[sl, :] = (_rms(x_ref[sl, :], NORM_EPS) * g_ref[...]).astype(BF16)
            return c
        lax.fori_loop(0, x_ref.shape[0] // strip, body, 0)

    o_ref[...] = jnp.dot(u_scr[...], w_ref[...], preferred_element_type=F32).astype(o_ref.dtype)


def _rms_inproj(x2d, g, w, *, tm, tn):
    m, d = x2d.shape
    n = w.shape[1]
    strip = min(128, tm)
    return pl.pallas_call(
        functools.partial(_rms_inproj_kernel, strip=strip),
        grid=(m // tm, n // tn),
        in_specs=[pl.BlockSpec((tm, d), lambda i, j: (i, 0)),
                  pl.BlockSpec((1, d), lambda i, j: (0, 0)),
                  pl.BlockSpec((d, tn), lambda i, j: (0, j))],
        out_specs=pl.BlockSpec((tm, tn), lambda i, j: (i, j)),
        out_shape=jax.ShapeDtypeStruct((m, n), BF16),
        scratch_shapes=[pltpu.VMEM((tm, d), BF16)],
        compiler_params=_params(2),
        name="rms_inproj",
    )(x2d, g, w)


def _matmul_kernel(a_ref, w_ref, o_ref):
    o_ref[...] = jnp.dot(a_ref[...], w_ref[...], preferred_element_type=F32).astype(o_ref.dtype)


def _matmul(a, w, *, tm, tn):
    m, d = a.shape
    n = w.shape[1]
    return pl.pallas_call(
        _matmul_kernel,
        grid=(m // tm, n // tn),
        in_specs=[pl.BlockSpec((tm, d), lambda i, j: (i, 0)),
                  pl.BlockSpec((d, tn), lambda i, j: (0, j))],
        out_specs=pl.BlockSpec((tm, tn), lambda i, j: (i, j)),
        out_shape=jax.ShapeDtypeStruct((m, n), BF16),
        compiler_params=_params(2),
        name="inproj",
    )(a, w)


def _gate_outproj_kernel(*refs, n_y, last):
    y_refs = refs[:n_y]
    gate_ref, w_ref, res_ref, g_ref = refs[n_y:n_y + 4]
    outs = refs[n_y + 4:]
    y = jnp.concatenate([r[...].astype(F32) for r in y_refs], axis=-1) if n_y > 1 else y_refs[0][...].astype(F32)
    gate = gate_ref[...].astype(F32)
    y = (y * (gate / (1.0 + jnp.exp(-gate)))).astype(BF16)
    h = jnp.dot(y, w_ref[...], preferred_element_type=F32) + res_ref[...]
    u = _rms(h, NORM_EPS) * g_ref[...]
    if last:
        outs[0][...] = u.astype(outs[0].dtype)
    else:
        outs[0][...] = h
        outs[1][...] = u.astype(outs[1].dtype)


def _gate_outproj(ys, proj, w, res, g_next, *, tm, last):
    m, d = res.shape
    k = w.shape[0]
    in_specs = [pl.BlockSpec((tm, y.shape[1]), lambda i: (i, 0)) for y in ys]
    in_specs += [pl.BlockSpec((tm, k), lambda i: (i, 0)),
                 pl.BlockSpec((k, d), lambda i: (0, 0)),
                 pl.BlockSpec((tm, d), lambda i: (i, 0)),
                 pl.BlockSpec((1, d), lambda i: (0, 0))]
    if last:
        out_shape = [jax.ShapeDtypeStruct((m, d), F32)]
    else:
        out_shape = [jax.ShapeDtypeStruct((m, d), F32), jax.ShapeDtypeStruct((m, d), BF16)]
    out_specs = [pl.BlockSpec((tm, d), lambda i: (i, 0)) for _ in out_shape]
    return pl.pallas_call(
        functools.partial(_gate_outproj_kernel, n_y=len(ys), last=last),
        grid=(m // tm,),
        in_specs=in_specs,
        out_specs=out_specs,
        out_shape=out_shape,
        compiler_params=_params(1),
        name="gate_outproj",
    )(*ys, proj, w, res, g_next)


def _attend(n_tiles, q_tile, k_ref, vt_ref, s_scr, bias, finalize):
    ns, nk, kc, _ = s_scr.shape
    ahead = ns - 1
    col_max = {}

    def scores(t, c):
        s = lax.dot_general(k_ref[c * kc:(c + 1) * kc, :], q_tile(t), (((1,), (1,)), ((), ())),
                            preferred_element_type=F32)
        if bias is not None:
            s = s + bias(t, c)
        s_scr[t % ns, c] = s
        cm = jnp.max(s, axis=0, keepdims=True)
        col_max[t] = cm if c == 0 else jnp.maximum(col_max[t], cm)

    for t in range(min(ahead, n_tiles)):
        for c in range(nk):
            scores(t, c)
    for t in range(n_tiles):
        m_cur = col_max.pop(t)
        l = acc = None
        for c in range(nk):
            if t + ahead < n_tiles:
                scores(t + ahead, c)
            e = jnp.exp2(s_scr[t % ns, c] - m_cur)
            ls = jnp.sum(e, axis=0, keepdims=True)
            pv = jnp.dot(vt_ref[:, c * kc:(c + 1) * kc], e.astype(BF16), preferred_element_type=F32)
            l = ls if l is None else l + ls
            acc = pv if acc is None else acc + pv
        finalize(t, acc, l)


def _transpose_into(vt_scr, v_ref):
    for r in range(v_ref.shape[0] // QT):
        vt_scr[:, r * QT:(r + 1) * QT] = v_ref[r * QT:(r + 1) * QT, :].astype(F32).T.astype(BF16)


def _build_bias_table(t_scr, s_len, f):
    def body(r, carry):
        u = lax.broadcasted_iota(jnp.int32, (QT, QT), 0) + (r * QT - s_len)
        c = lax.broadcasted_iota(jnp.int32, (QT, QT), 1)
        t_scr[pl.ds(pl.multiple_of(r * QT, QT), QT), :] = f(u - c)
        return carry
    lax.fori_loop(0, t_scr.shape[0] // QT, body, 0)


def _norm_rope(x, g, cos, sin_up, sin_dn, scale):
    y = _rms(x.astype(F32), NORM_EPS) * g
    out = y * cos + pltpu.roll(y, 96, 1) * sin_up + pltpu.roll(y, 32, 1) * sin_dn
    if scale != 1.0:
        out = out * scale
    return out.astype(BF16)


def _rope_tables(s):
    pos = jnp.arange(s)
    row_ids = (pos // GRID_W).astype(F32)
    col_ids = (pos % GRID_W).astype(F32)
    inv_freq = ROPE_THETA ** (-jnp.arange(ROPE_PAIRS, dtype=F32) / ROPE_PAIRS)
    ang_r = row_ids[:, None] * inv_freq[None, :]
    ang_c = col_ids[:, None] * inv_freq[None, :]
    cr, sr, cc, sc = jnp.cos(ang_r), jnp.sin(ang_r), jnp.cos(ang_c), jnp.sin(ang_c)
    z = jnp.zeros_like(sr)
    cos = jnp.concatenate([cr, cr, cc, cc], axis=-1)
    sin_up = jnp.concatenate([-sr, z, -sc, z], axis=-1)
    sin_dn = jnp.concatenate([z, sr, z, sc], axis=-1)
    return cos, sin_up, sin_dn


def _slope(h, n_heads):
    hv = jnp.full((1, 1), h + 1, jnp.int32).astype(F32)
    return jnp.exp2(-8.0 * hv / n_heads)


def _gqa_kernel(q_ref, k_ref, v_ref, gq_ref, gk_ref, cos_ref, sup_ref, sdn_ref, o_ref,
                k_scr, vt_scr, q_scr, s_scr, *, strip):
    qi = pl.program_id(2)
    s_len = k_ref.shape[0]

    @pl.when(qi == 0)
    def _():
        def body(r, c):
            sl = pl.ds(pl.multiple_of(r * strip, strip), strip)
            k_scr[sl, :] = _norm_rope(k_ref[sl, :], gk_ref[...], cos_ref[sl, :], sup_ref[sl, :],
                                      sdn_ref[sl, :], 1.0)
            return c
        lax.fori_loop(0, s_len // strip, body, 0)
        _transpose_into(vt_scr, v_ref)

    rows = pl.ds(pl.multiple_of(qi * QT, QT), QT)
    cos, sup, sdn = cos_ref[rows, :], sup_ref[rows, :], sdn_ref[rows, :]
    scale = LOG2E / math.sqrt(HEAD_DIM)
    for g in range(A_GROUP):
        q_scr[g * QT:(g + 1) * QT, :] = _norm_rope(q_ref[:, g * HEAD_DIM:(g + 1) * HEAD_DIM], gq_ref[...],
                                                   cos, sup, sdn, scale)

    def finalize(t, acc, l):
        o_ref[:, t * HEAD_DIM:(t + 1) * HEAD_DIM] = (acc * (1.0 / l)).T.astype(o_ref.dtype)

    _attend(A_GROUP, lambda t: q_scr[t * QT:(t + 1) * QT, :], k_scr, vt_scr, s_scr, None, finalize)


def _gqa_attention(proj, gq, gk, tables, *, b, s, kc):
    nq = s // QT
    gw = A_GROUP * HEAD_DIM
    full = lambda bi, kv, qi: (0, 0)
    return pl.pallas_call(
        functools.partial(_gqa_kernel, strip=min(256, s)),
        grid=(b, A_KV_HEADS, nq),
        in_specs=[pl.BlockSpec((QT, gw), lambda bi, kv, qi: (bi * nq + qi, EV_QA // gw + kv)),
                  pl.BlockSpec((s, HEAD_DIM), lambda bi, kv, qi: (bi, EV_KA // HEAD_DIM + kv)),
                  pl.BlockSpec((s, HEAD_DIM), lambda bi, kv, qi: (bi, EV_VA // HEAD_DIM + kv)),
                  pl.BlockSpec((1, HEAD_DIM), full),
                  pl.BlockSpec((1, HEAD_DIM), full),
                  pl.BlockSpec((s, HEAD_DIM), full),
                  pl.BlockSpec((s, HEAD_DIM), full),
                  pl.BlockSpec((s, HEAD_DIM), full)],
        out_specs=pl.BlockSpec((QT, gw), lambda bi, kv, qi: (bi * nq + qi, kv)),
        out_shape=jax.ShapeDtypeStruct((b * s, A_Q), BF16),
        scratch_shapes=[pltpu.VMEM((s, HEAD_DIM), BF16),
                        pltpu.VMEM((HEAD_DIM, s), BF16),
                        pltpu.VMEM((A_GROUP * QT, HEAD_DIM), BF16),
                        pltpu.VMEM((SCORE_SLOTS, s // kc, kc, QT), F32)],
        compiler_params=_params(3),
        name="gqa_attention",
    )(proj, proj, proj, gq, gk, *tables)


def _diff_kernel(q_ref, k_ref, v_ref, lq1_ref, lk1_ref, lq2_ref, lk2_ref, g_ref, o_ref,
                 q_scr, vt_scr, s_scr, t_scr, *, tq, lambda_init):
    h, b, qi = pl.program_id(0), pl.program_id(1), pl.program_id(2)
    s_len = k_ref.shape[0]
    kc = s_scr.shape[2]

    @pl.when(jnp.logical_and(b == 0, qi == 0))
    def _():
        slope = _slope(h, B_HEADS) * LOG2E
        _build_bias_table(t_scr, s_len, lambda d: -slope * jnp.abs(d.astype(F32)))

    @pl.when(qi == 0)
    def _():
        _transpose_into(vt_scr, v_ref)

    lane = lax.broadcasted_iota(jnp.int32, (tq, HEAD_DIM), 1)
    qf = q_ref[...].astype(F32) * (LOG2E / math.sqrt(B_QK_DIM))
    q_scr[0:tq, :] = jnp.where(lane < B_QK_DIM, qf, 0.0).astype(BF16)
    q_scr[tq:2 * tq, :] = jnp.where(lane < B_QK_DIM, 0.0, qf).astype(BF16)
    lam = (jnp.exp(jnp.sum(lq1_ref[...] * lk1_ref[...], axis=-1, keepdims=True))
           - jnp.exp(jnp.sum(lq2_ref[...] * lk2_ref[...], axis=-1, keepdims=True)) + lambda_init)

    def q_tile(t):
        r0 = (t % 2) * tq + (t // 2) * QT
        return q_scr[r0:r0 + QT, :]

    def bias(t, c):
        u0 = c * kc - (qi * tq + (t // 2) * QT) + s_len
        return t_scr[pl.ds(pl.multiple_of(u0, QT), kc), :]

    first_map = {}

    def finalize(t, acc, l):
        o = acc * (1.0 / l)
        if t % 2 == 0:
            first_map["o"] = o
            return
        o = first_map["o"] - lam * o
        o = o * lax.rsqrt(jnp.mean(o * o, axis=0, keepdims=True) + SUBLN_EPS) * g_ref[...] * (1.0 - lambda_init)
        o_ref[(t // 2) * QT:(t // 2 + 1) * QT, :] = o.T.astype(o_ref.dtype)

    _attend(2 * (tq // QT), q_tile, k_ref, vt_scr, s_scr, bias, finalize)


def _diff_attention(proj, lq1, lk1, lq2, lk2, g_col, *, b, s, tq, kc, lambda_init):
    nq = s // tq
    full = lambda h, bi, qi: (0, 0)
    return pl.pallas_call(
        functools.partial(_diff_kernel, tq=tq, lambda_init=lambda_init),
        grid=(B_HEADS, b, nq),
        in_specs=[pl.BlockSpec((tq, HEAD_DIM), lambda h, bi, qi: (bi * nq + qi, EV_QB // HEAD_DIM + h)),
                  pl.BlockSpec((s, HEAD_DIM), lambda h, bi, qi: (bi, EV_KB // HEAD_DIM + h)),
                  pl.BlockSpec((s, HEAD_DIM), lambda h, bi, qi: (bi, EV_VB // HEAD_DIM + h)),
                  pl.BlockSpec((1, B_QK_DIM), full),
                  pl.BlockSpec((1, B_QK_DIM), full),
                  pl.BlockSpec((1, B_QK_DIM), full),
                  pl.BlockSpec((1, B_QK_DIM), full),
                  pl.BlockSpec((HEAD_DIM, 1), full)],
        out_specs=pl.BlockSpec((tq, HEAD_DIM), lambda h, bi, qi: (bi * nq + qi, h)),
        out_shape=jax.ShapeDtypeStruct((b * s, B_W), BF16),
        scratch_shapes=[pltpu.VMEM((2 * tq, HEAD_DIM), BF16),
                        pltpu.VMEM((HEAD_DIM, s), BF16),
                        pltpu.VMEM((SCORE_SLOTS, s // kc, kc, QT), F32),
                        pltpu.VMEM((2 * s, QT), F32)],
        compiler_params=_params(3),
        name="diff_attention",
    )(proj, proj, proj, lq1, lk1, lq2, lk2, g_col)


def _dilated_bias(d, slope):
    ad = jnp.abs(d.astype(F32))
    cnt = jnp.zeros(d.shape, F32)
    for window, dil in C_PATTERNS:
        cnt = cnt + jnp.where(((d & (dil - 1)) == 0) & (ad <= float(window // 2)), 1.0, 0.0)
    log_cnt = jnp.where(cnt > 2.5, math.log2(3.0), jnp.where(cnt > 1.5, 1.0, 0.0))
    return jnp.where(cnt > 0.5, log_cnt - slope * ad, NEG_INF)


def _dilated_kernel(q_ref, k_ref, v_ref, o_ref, q_scr, vt_scr, s_scr, t_scr, *, tq):
    h, b, qi = pl.program_id(0), pl.program_id(1), pl.program_id(2)
    s_len = k_ref.shape[0]
    kc = s_scr.shape[2]

    @pl.when(jnp.logical_and(b == 0, qi == 0))
    def _():
        slope = _slope(h, C_HEADS) * LOG2E
        _build_bias_table(t_scr, s_len, lambda d: _dilated_bias(d, slope))

    @pl.when(qi == 0)
    def _():
        _transpose_into(vt_scr, v_ref)

    q_scr[...] = (q_ref[...].astype(F32) * (LOG2E / math.sqrt(HEAD_DIM))).astype(BF16)

    def bias(t, c):
        u0 = c * kc - (qi * tq + t * QT) + s_len
        return t_scr[pl.ds(pl.multiple_of(u0, QT), kc), :]

    def finalize(t, acc, l):
        o_ref[t * QT:(t + 1) * QT, :] = (acc * (1.0 / l)).T.astype(o_ref.dtype)

    _attend(tq // QT, lambda t: q_scr[t * QT:(t + 1) * QT, :], k_ref, vt_scr, s_scr, bias, finalize)


def _dilated_attention(proj, *, b, s, tq, kc):
    nq = s // tq
    return pl.pallas_call(
        functools.partial(_dilated_kernel, tq=tq),
        grid=(C_HEADS, b, nq),
        in_specs=[pl.BlockSpec((tq, HEAD_DIM), lambda h, bi, qi: (bi * nq + qi, OD_Q // HEAD_DIM + h)),
                  pl.BlockSpec((s, HEAD_DIM), lambda h, bi, qi: (bi, OD_K // HEAD_DIM + h)),
                  pl.BlockSpec((s, HEAD_DIM), lambda h, bi, qi: (bi, OD_V // HEAD_DIM + h))],
        out_specs=pl.BlockSpec((tq, HEAD_DIM), lambda h, bi, qi: (bi * nq + qi, h)),
        out_shape=jax.ShapeDtypeStruct((b * s, C_W), BF16),
        scratch_shapes=[pltpu.VMEM((tq, HEAD_DIM), BF16),
                        pltpu.VMEM((HEAD_DIM, s), BF16),
                        pltpu.VMEM((SCORE_SLOTS, s // kc, kc, QT), F32),
                        pltpu.VMEM((2 * s, QT), F32)],
        compiler_params=_params(3),
        name="dilated_attention",
    )(proj, proj, proj)


def _largest_tile(n, cap):
    t = min(n, cap)
    while n % t:
        t //= 2
    return t


def kernel(x, ln_even_g, w_in_even, a_q_norm_g, a_k_norm_g, b_lambda_q1, b_lambda_k1, b_lambda_q2,
           b_lambda_k2, b_subln_g, w_out_even, ln_odd_g, w_in_odd, w_out_odd, final_norm_g):
    b, s, d = x.shape
    m = b * s
    depth = ln_even_g.shape[0] + ln_odd_g.shape[0]
    assert s % GRID_W == 0 and s % QT == 0 and d == MIX
    tq = _largest_tile(s, 1024)
    kc = _largest_tile(s, 512)
    tables = _rope_tables(s)
    row = lambda v: v.reshape(1, -1).astype(F32)

    def norm_gain(layer):
        if layer == depth:
            return row(final_norm_g)
        return row(ln_even_g[layer // 2] if layer % 2 == 0 else ln_odd_g[layer // 2])

    h = x.reshape(m, d)
    u = None
    for layer in range(depth):
        i = layer // 2
        last = layer == depth - 1
        if layer % 2 == 0:
            w = w_in_even[i]
            w = jnp.concatenate([w[:, IN_EVEN - MIX:], w[:, :IN_EVEN - MIX]], axis=1).astype(BF16)
            if u is None:
                proj = _rms_inproj(h, norm_gain(layer), w, tm=_largest_tile(m, 1024), tn=IN_EVEN // 4)
            else:
                proj = _matmul(u, w, tm=_largest_tile(m, 1024), tn=IN_EVEN // 4)
            ya = _gqa_attention(proj, row(a_q_norm_g[i]), row(a_k_norm_g[i]), tables, b=b, s=s, kc=kc)
            lambda_init = 0.8 - 0.6 * math.exp(-0.3 * layer)
            yb = _diff_attention(proj, row(b_lambda_q1[i]), row(b_lambda_k1[i]), row(b_lambda_q2[i]),
                                 row(b_lambda_k2[i]), b_subln_g[i].reshape(-1, 1).astype(F32),
                                 b=b, s=s, tq=tq, kc=kc, lambda_init=lambda_init)
            ys, w_out = [ya, yb], w_out_even[i]
        else:
            w = w_in_odd[i]
            w = jnp.concatenate([w[:, IN_ODD - C_W:], w[:, :IN_ODD - C_W]], axis=1).astype(BF16)
            if u is None:
                proj = _rms_inproj(h, norm_gain(layer), w, tm=_largest_tile(m, 1024), tn=IN_ODD // 4)
            else:
                proj = _matmul(u, w, tm=_largest_tile(m, 1024), tn=IN_ODD // 4)
            ys, w_out = [_dilated_attention(proj, b=b, s=s, tq=tq, kc=kc)], w_out_odd[i]
        outs = _gate_outproj(ys, proj, w_out.astype(BF16), h, norm_gain(layer + 1),
                             tm=_largest_tile(m, 256), last=last)
        if last:
            return outs[0].reshape(b, s, d)
        h, u = outs
```

```python
import functools
import math

import jax
import jax.numpy as jnp
from jax import lax
from jax.experimental import pallas as pl
from jax.experimental.pallas import tpu as pltpu

F32 = jnp.float32
BF16 = jnp.bfloat16

HEAD_DIM = 128
A_HEADS = 8
A_KV_HEADS = 2
A_GROUP = A_HEADS // A_KV_HEADS
B_HEADS = 8
B_QK_DIM = 64
C_HEADS = 16
C_PATTERNS = ((128, 1), (512, 4), (2048, 16))
GRID_W = 64
ROPE_THETA = 10000.0
ROPE_PAIRS = HEAD_DIM // 4
NORM_EPS = 1e-6
SUBLN_EPS = 1e-5
NEG_INF = -1e30
LOG2E = math.log2(math.e)
QT = 256
ONES_ROWS = 16
SCORE_SLOTS = 3

A_Q = A_HEADS * HEAD_DIM
A_KV = A_KV_HEADS * HEAD_DIM
B_W = B_HEADS * HEAD_DIM
MIX = A_Q + B_W
EV_GATE, EV_QA, EV_KA, EV_VA = 0, MIX, MIX + A_Q, MIX + A_Q + A_KV
EV_QB = EV_VA + A_KV
EV_KB = EV_QB + B_W
EV_VB = EV_KB + B_W
IN_EVEN = EV_VB + B_W
C_W = C_HEADS * HEAD_DIM
OD_GATE, OD_Q, OD_K, OD_V = 0, C_W, 2 * C_W, 3 * C_W
IN_ODD = 4 * C_W

V7X_VMEM_LIMIT_BYTES = 56 * 1024 * 1024


def _params(n_axes):
    return pltpu.CompilerParams(dimension_semantics=("arbitrary",) * n_axes,
                                vmem_limit_bytes=V7X_VMEM_LIMIT_BYTES)


def _rms(x, eps):
    return x * lax.rsqrt(jnp.mean(x * x, axis=-1, keepdims=True) + eps)


def _rms_inproj_kernel(x_ref, g_ref, w_ref, o_ref, u_scr, *, strip):
    @pl.when(pl.program_id(1) == 0)
    def _():
        def body(r, c):
            sl = pl.ds(pl.multiple_of(r * strip, strip), strip)
            u_scr[sl, :] = (_rms(x_ref[sl, :], NORM_EPS) * g_ref[...]).astype(BF16)
            return c
        lax.fori_loop(0, x_ref.shape[0] // strip, body, 0)

    o_ref[...] = jnp.dot(u_scr[...], w_ref[...], preferred_element_type=F32).astype(o_ref.dtype)


def _rms_inproj(x2d, g, w, *, tm, tn):
    m, d = x2d.shape
    n = w.shape[1]
    strip = min(128, tm)
    return pl.pallas_call(
        functools.partial(_rms_inproj_kernel, strip=strip),
        grid=(m // tm, n // tn),
        in_specs=[pl.BlockSpec((tm, d), lambda i, j: (i, 0)),
                  pl.BlockSpec((1, d), lambda i, j: (0, 0)),
                  pl.BlockSpec((d, tn), lambda i, j: (0, j))],
        out_specs=pl.BlockSpec((tm, tn), lambda i, j: (i, j)),
        out_shape=jax.ShapeDtypeStruct((m, n), BF16),
        scratch_shapes=[pltpu.VMEM((tm, d), BF16)],
        compiler_params=_params(2),
        name="rms_inproj",
    )(x2d, g, w)


def _matmul_kernel(a_ref, w_ref, o_ref):
    o_ref[...] = jnp.dot(a_ref[...], w_ref[...], preferred_element_type=F32).astype(o_ref.dtype)


def _matmul(a, w, *, tm, tn):
    m, d = a.shape
    n = w.shape[1]
    return pl.pallas_call(
        _matmul_kernel,
        grid=(m // tm, n // tn),
        in_specs=[pl.BlockSpec((tm, d), lambda i, j: (i, 0)),
                  pl.BlockSpec((d, tn), lambda i, j: (0, j))],
        out_specs=pl.BlockSpec((tm, tn), lambda i, j: (i, j)),
        out_shape=jax.ShapeDtypeStruct((m, n), BF16),
        compiler_params=_params(2),
        name="inproj",
    )(a, w)


def _gate_outproj_kernel(*refs, n_y, last):
    y_refs = refs[:n_y]
    gate_ref, w_ref, res_ref, g_ref = refs[n_y:n_y + 4]
    outs = refs[n_y + 4:]
    y = jnp.concatenate([r[...].astype(F32) for r in y_refs], axis=-1) if n_y > 1 else y_refs[0][...].astype(F32)
    gate = gate_ref[...].astype(F32)
    y = (y * (gate / (1.0 + jnp.exp(-gate)))).astype(BF16)
    h = jnp.dot(y, w_ref[...], preferred_element_type=F32) + res_ref[...]
    u = _rms(h, NORM_EPS) * g_ref[...]
    if last:
        outs[0][...] = u.astype(outs[0].dtype)
    else:
        outs[0][...] = h
        outs[1][...] = u.astype(outs[1].dtype)


def _gate_outproj(ys, proj, w, res, g_next, *, tm, last):
    m, d = res.shape
    k = w.shape[0]
    in_specs = [pl.BlockSpec((tm, y.shape[1]), lambda i: (i, 0)) for y in ys]
    in_specs += [pl.BlockSpec((tm, k), lambda i: (i, 0)),
                 pl.BlockSpec((k, d), lambda i: (0, 0), pipeline_mode=pl.Buffered(1)),
                 pl.BlockSpec((tm, d), lambda i: (i, 0)),
                 pl.BlockSpec((1, d), lambda i: (0, 0))]
    if last:
        out_shape = [jax.ShapeDtypeStruct((m, d), F32)]
    else:
        out_shape = [jax.ShapeDtypeStruct((m, d), F32), jax.ShapeDtypeStruct((m, d), BF16)]
    out_specs = [pl.BlockSpec((tm, d), lambda i: (i, 0)) for _ in out_shape]
    return pl.pallas_call(
        functools.partial(_gate_outproj_kernel, n_y=len(ys), last=last),
        grid=(m // tm,),
        in_specs=in_specs,
        out_specs=out_specs,
        out_shape=out_shape,
        compiler_params=_params(1),
        name="gate_outproj",
    )(*ys, proj, w, res, g_next)


def _attend(n_tiles, q_tile, k_ref, vt_ref, s_scr, bias, finalize, chunks=None):
    ns, nk, kc, _ = s_scr.shape
    ahead = ns - 1
    sum_on_mxu = vt_ref.shape[0] > HEAD_DIM
    if chunks is None:
        chunks = lambda t: range(nk)
    col_max = {}

    def scores(t, c):
        s = lax.dot_general(k_ref[c * kc:(c + 1) * kc, :], q_tile(t), (((1,), (1,)), ((), ())),
                            preferred_element_type=F32)
        if bias is not None:
            s = s + bias(t, c)
        s_scr[t % ns, c] = s
        cm = jnp.max(s, axis=0, keepdims=True)
        col_max[t] = jnp.maximum(col_max[t], cm) if t in col_max else cm

    for t in range(min(ahead, n_tiles)):
        for c in chunks(t):
            scores(t, c)
    for t in range(n_tiles):
        m_cur = col_max.pop(t)
        l = acc = None
        later = list(chunks(t + ahead)) if t + ahead < n_tiles else []
        for i, c in enumerate(chunks(t)):
            if i < len(later):
                scores(t + ahead, later[i])
            x = s_scr[t % ns, c] - m_cur
            if sum_on_mxu:
                e = jnp.exp2(x.astype(BF16))
            else:
                e = jnp.exp2(x)
                ls = jnp.sum(e, axis=0, keepdims=True)
                l = ls if l is None else l + ls
                e = e.astype(BF16)
            pv = jnp.dot(vt_ref[:, c * kc:(c + 1) * kc], e, preferred_element_type=F32)
            acc = pv if acc is None else acc + pv
        for c in later[len(list(chunks(t))):]:
            scores(t + ahead, c)
        if sum_on_mxu:
            acc, l = acc[0:HEAD_DIM], acc[HEAD_DIM:HEAD_DIM + 1]
        finalize(t, acc, l)


def _transpose_into(vt_scr, v_ref):
    s_len = v_ref.shape[0]
    for r in range(s_len // QT):
        vt_scr[0:HEAD_DIM, r * QT:(r + 1) * QT] = v_ref[r * QT:(r + 1) * QT, :].astype(F32).T.astype(BF16)
    if vt_scr.shape[0] > HEAD_DIM:
        vt_scr[HEAD_DIM:, :] = jnp.ones((vt_scr.shape[0] - HEAD_DIM, s_len), BF16)


def _build_bias_table(t_scr, s_len, f):
    def body(r, carry):
        u = lax.broadcasted_iota(jnp.int32, (QT, QT), 0) + (r * QT - s_len)
        c = lax.broadcasted_iota(jnp.int32, (QT, QT), 1)
        t_scr[pl.ds(pl.multiple_of(r * QT, QT), QT), :] = f(u - c)
        return carry
    lax.fori_loop(0, t_scr.shape[0] // QT, body, 0)


def _norm_rope(x, g, cos, sin_up, sin_dn, scale):
    y = _rms(x.astype(F32), NORM_EPS) * g
    out = y * cos + pltpu.roll(y, 96, 1) * sin_up + pltpu.roll(y, 32, 1) * sin_dn
    if scale != 1.0:
        out = out * scale
    return out.astype(BF16)


def _rope_tables(s):
    pos = jnp.arange(s)
    row_ids = (pos // GRID_W).astype(F32)
    col_ids = (pos % GRID_W).astype(F32)
    inv_freq = ROPE_THETA ** (-jnp.arange(ROPE_PAIRS, dtype=F32) / ROPE_PAIRS)
    ang_r = row_ids[:, None] * inv_freq[None, :]
    ang_c = col_ids[:, None] * inv_freq[None, :]
    cr, sr, cc, sc = jnp.cos(ang_r), jnp.sin(ang_r), jnp.cos(ang_c), jnp.sin(ang_c)
    z = jnp.zeros_like(sr)
    cos = jnp.concatenate([cr, cr, cc, cc], axis=-1)
    sin_up = jnp.concatenate([-sr, z, -sc, z], axis=-1)
    sin_dn = jnp.concatenate([z, sr, z, sc], axis=-1)
    return cos, sin_up, sin_dn


def _slope(h, n_heads):
    hv = jnp.full((1, 1), h + 1, jnp.int32).astype(F32)
    return jnp.exp2(-8.0 * hv / n_heads)


def _gqa_kernel(q_ref, k_ref, v_ref, gq_ref, gk_ref, cos_ref, sup_ref, sdn_ref, o_ref,
                k_scr, vt_scr, q_scr, s_scr, *, strip):
    qi = pl.program_id(2)
    s_len = k_ref.shape[0]

    @pl.when(qi == 0)
    def _():
        def body(r, c):
            sl = pl.ds(pl.multiple_of(r * strip, strip), strip)
            k_scr[sl, :] = _norm_rope(k_ref[sl, :], gk_ref[...], cos_ref[sl, :], sup_ref[sl, :],
                                      sdn_ref[sl, :], 1.0)
            return c
        lax.fori_loop(0, s_len // strip, body, 0)
        _transpose_into(vt_scr, v_ref)

    rows = pl.ds(pl.multiple_of(qi * QT, QT), QT)
    cos, sup, sdn = cos_ref[rows, :], sup_ref[rows, :], sdn_ref[rows, :]
    scale = LOG2E / math.sqrt(HEAD_DIM)
    for g in range(A_GROUP):
        q_scr[g * QT:(g + 1) * QT, :] = _norm_rope(q_ref[:, g * HEAD_DIM:(g + 1) * HEAD_DIM], gq_ref[...],
                                                   cos, sup, sdn, scale)

    def finalize(t, acc, l):
        o_ref[:, t * HEAD_DIM:(t + 1) * HEAD_DIM] = (acc * (1.0 / l)).T.astype(o_ref.dtype)

    _attend(A_GROUP, lambda t: q_scr[t * QT:(t + 1) * QT, :], k_scr, vt_scr, s_scr, None, finalize)


def _gqa_attention(proj, gq, gk, tables, *, b, s, kc):
    nq = s // QT
    gw = A_GROUP * HEAD_DIM
    full = lambda bi, kv, qi: (0, 0)
    return pl.pallas_call(
        functools.partial(_gqa_kernel, strip=min(256, s)),
        grid=(b, A_KV_HEADS, nq),
        in_specs=[pl.BlockSpec((QT, gw), lambda bi, kv, qi: (bi * nq + qi, EV_QA // gw + kv)),
                  pl.BlockSpec((s, HEAD_DIM), lambda bi, kv, qi: (bi, EV_KA // HEAD_DIM + kv)),
                  pl.BlockSpec((s, HEAD_DIM), lambda bi, kv, qi: (bi, EV_VA // HEAD_DIM + kv)),
                  pl.BlockSpec((1, HEAD_DIM), full),
                  pl.BlockSpec((1, HEAD_DIM), full),
                  pl.BlockSpec((s, HEAD_DIM), full),
                  pl.BlockSpec((s, HEAD_DIM), full),
                  pl.BlockSpec((s, HEAD_DIM), full)],
        out_specs=pl.BlockSpec((QT, gw), lambda bi, kv, qi: (bi * nq + qi, kv)),
        out_shape=jax.ShapeDtypeStruct((b * s, A_Q), BF16),
        scratch_shapes=[pltpu.VMEM((s, HEAD_DIM), BF16),
                        pltpu.VMEM((HEAD_DIM, s), BF16),
                        pltpu.VMEM((A_GROUP * QT, HEAD_DIM), BF16),
                        pltpu.VMEM((SCORE_SLOTS, s // kc, kc, QT), F32)],
        compiler_params=_params(3),
        name="gqa_attention",
    )(proj, proj, proj, gq, gk, *tables)


def _diff_kernel(q_ref, k_ref, v_ref, lq1_ref, lk1_ref, lq2_ref, lk2_ref, g_ref, o_ref,
                 q_scr, vt_scr, s_scr, t_scr, *, tq, lambda_init):
    h, b, qi = pl.program_id(0), pl.program_id(1), pl.program_id(2)
    s_len = k_ref.shape[0]
    kc = s_scr.shape[2]

    @pl.when(jnp.logical_and(b == 0, qi == 0))
    def _():
        slope = _slope(h, B_HEADS) * LOG2E
        _build_bias_table(t_scr, s_len, lambda d: -slope * jnp.abs(d.astype(F32)))

    @pl.when(qi == 0)
    def _():
        _transpose_into(vt_scr, v_ref)

    lane = lax.broadcasted_iota(jnp.int32, (tq, HEAD_DIM), 1)
    qf = q_ref[...].astype(F32) * (LOG2E / math.sqrt(B_QK_DIM))
    q_scr[0:tq, :] = jnp.where(lane < B_QK_DIM, qf, 0.0).astype(BF16)
    q_scr[tq:2 * tq, :] = jnp.where(lane < B_QK_DIM, 0.0, qf).astype(BF16)
    lam = (jnp.exp(jnp.sum(lq1_ref[...] * lk1_ref[...], axis=-1, keepdims=True))
           - jnp.exp(jnp.sum(lq2_ref[...] * lk2_ref[...], axis=-1, keepdims=True)) + lambda_init)

    def q_tile(t):
        r0 = (t % 2) * tq + (t // 2) * QT
        return q_scr[r0:r0 + QT, :]

    def bias(t, c):
        u0 = c * kc - (qi * tq + (t // 2) * QT) + s_len
        return t_scr[pl.ds(pl.multiple_of(u0, QT), kc), :]

    first_map = {}

    def finalize(t, acc, l):
        o = acc * (1.0 / l)
        if t % 2 == 0:
            first_map["o"] = o
            return
        o = first_map["o"] - lam * o
        o = o * lax.rsqrt(jnp.mean(o * o, axis=0, keepdims=True) + SUBLN_EPS) * g_ref[...] * (1.0 - lambda_init)
        o_ref[(t // 2) * QT:(t // 2 + 1) * QT, :] = o.T.astype(o_ref.dtype)

    _attend(2 * (tq // QT), q_tile, k_ref, vt_scr, s_scr, bias, finalize)


def _diff_attention(proj, lq1, lk1, lq2, lk2, g_col, *, b, s, tq, kc, lambda_init):
    nq = s // tq
    full = lambda h, bi, qi: (0, 0)
    return pl.pallas_call(
        functools.partial(_diff_kernel, tq=tq, lambda_init=lambda_init),
        grid=(B_HEADS, b, nq),
        in_specs=[pl.BlockSpec((tq, HEAD_DIM), lambda h, bi, qi: (bi * nq + qi, EV_QB // HEAD_DIM + h)),
                  pl.BlockSpec((s, HEAD_DIM), lambda h, bi, qi: (bi, EV_KB // HEAD_DIM + h)),
                  pl.BlockSpec((s, HEAD_DIM), lambda h, bi, qi: (bi, EV_VB // HEAD_DIM + h)),
                  pl.BlockSpec((1, B_QK_DIM), full),
                  pl.BlockSpec((1, B_QK_DIM), full),
                  pl.BlockSpec((1, B_QK_DIM), full),
                  pl.BlockSpec((1, B_QK_DIM), full),
                  pl.BlockSpec((HEAD_DIM, 1), full)],
        out_specs=pl.BlockSpec((tq, HEAD_DIM), lambda h, bi, qi: (bi * nq + qi, h)),
        out_shape=jax.ShapeDtypeStruct((b * s, B_W), BF16),
        scratch_shapes=[pltpu.VMEM((2 * tq, HEAD_DIM), BF16),
                        pltpu.VMEM((HEAD_DIM + ONES_ROWS, s), BF16),
                        pltpu.VMEM((SCORE_SLOTS, s // kc, kc, QT), F32),
                        pltpu.VMEM((2 * s, QT), F32)],
        compiler_params=_params(3),
        name="diff_attention",
    )(proj, proj, proj, lq1, lk1, lq2, lk2, g_col)


def _dilated_bias(d, slope):
    ad = jnp.abs(d.astype(F32))
    cnt = jnp.zeros(d.shape, F32)
    for window, dil in C_PATTERNS:
        cnt = cnt + jnp.where(((d & (dil - 1)) == 0) & (ad <= float(window // 2)), 1.0, 0.0)
    log_cnt = jnp.where(cnt > 2.5, math.log2(3.0), jnp.where(cnt > 1.5, 1.0, 0.0))
    return jnp.where(cnt > 0.5, log_cnt - slope * ad, NEG_INF)


def _dilated_kernel(q_ref, k_ref, v_ref, o_ref, q_scr, vt_scr, s_scr, t_scr):
    h, b = pl.program_id(0), pl.program_id(1)
    s_len = k_ref.shape[0]
    kc = s_scr.shape[2]
    reach = max(window // 2 for window, _ in C_PATTERNS)

    @pl.when(b == 0)
    def _():
        slope = _slope(h, C_HEADS) * LOG2E
        _build_bias_table(t_scr, s_len, lambda d: _dilated_bias(d, slope))

    _transpose_into(vt_scr, v_ref)
    q_scr[...] = (q_ref[...].astype(F32) * (LOG2E / math.sqrt(HEAD_DIM))).astype(BF16)

    def chunks(t):
        q_lo, q_hi = t * QT, (t + 1) * QT - 1
        return [c for c in range(s_len // kc)
                if c * kc - q_hi <= reach and q_lo - ((c + 1) * kc - 1) <= reach]

    def bias(t, c):
        u0 = c * kc - t * QT + s_len
        return t_scr[u0:u0 + kc, :]

    def finalize(t, acc, l):
        o_ref[t * QT:(t + 1) * QT, :] = (acc * (1.0 / l)).T.astype(o_ref.dtype)

    _attend(s_len // QT, lambda t: q_scr[t * QT:(t + 1) * QT, :], k_ref, vt_scr, s_scr, bias, finalize, chunks)


def _dilated_attention(proj, *, b, s, kc):
    return pl.pallas_call(
        _dilated_kernel,
        grid=(C_HEADS, b),
        in_specs=[pl.BlockSpec((s, HEAD_DIM), lambda h, bi: (bi, OD_Q // HEAD_DIM + h)),
                  pl.BlockSpec((s, HEAD_DIM), lambda h, bi: (bi, OD_K // HEAD_DIM + h)),
                  pl.BlockSpec((s, HEAD_DIM), lambda h, bi: (bi, OD_V // HEAD_DIM + h))],
        out_specs=pl.BlockSpec((s, HEAD_DIM), lambda h, bi: (bi, h)),
        out_shape=jax.ShapeDtypeStruct((b * s, C_W), BF16),
        scratch_shapes=[pltpu.VMEM((s, HEAD_DIM), BF16),
                        pltpu.VMEM((HEAD_DIM + ONES_ROWS, s), BF16),
                        pltpu.VMEM((SCORE_SLOTS, s // kc, kc, QT), F32),
                        pltpu.VMEM((2 * s, QT), F32)],
        compiler_params=_params(2),
        name="dilated_attention",
    )(proj, proj, proj)


def _largest_tile(n, cap):
    t = min(n, cap)
    while n % t:
        t //= 2
    return t


def kernel(x, ln_even_g, w_in_even, a_q_norm_g, a_k_norm_g, b_lambda_q1, b_lambda_k1, b_lambda_q2,
           b_lambda_k2, b_subln_g, w_out_even, ln_odd_g, w_in_odd, w_out_odd, final_norm_g):
    b, s, d = x.shape
    m = b * s
    depth = ln_even_g.shape[0] + ln_odd_g.shape[0]
    assert s % GRID_W == 0 and s % QT == 0 and d == MIX
    tq = _largest_tile(s, 1024)
    kc = _largest_tile(s, 512)
    tables = _rope_tables(s)
    row = lambda v: v.reshape(1, -1).astype(F32)

    def norm_gain(layer):
        if layer == depth:
            return row(final_norm_g)
        return row(ln_even_g[layer // 2] if layer % 2 == 0 else ln_odd_g[layer // 2])

    h = x.reshape(m, d)
    u = None
    for layer in range(depth):
        i = layer // 2
        last = layer == depth - 1
        if layer % 2 == 0:
            w = w_in_even[i]
            w = jnp.concatenate([w[:, IN_EVEN - MIX:], w[:, :IN_EVEN - MIX]], axis=1).astype(BF16)
            if u is None:
                proj = _rms_inproj(h, norm_gain(layer), w, tm=_largest_tile(m, 1024), tn=IN_EVEN // 4)
            else:
                proj = _matmul(u, w, tm=_largest_tile(m, 1024), tn=IN_EVEN // 4)
            ya = _gqa_attention(proj, row(a_q_norm_g[i]), row(a_k_norm_g[i]), tables, b=b, s=s, kc=kc)
            lambda_init = 0.8 - 0.6 * math.exp(-0.3 * layer)
            yb = _diff_attention(proj, row(b_lambda_q1[i]), row(b_lambda_k1[i]), row(b_lambda_q2[i]),
                                 row(b_lambda_k2[i]), b_subln_g[i].reshape(-1, 1).astype(F32),
                                 b=b, s=s, tq=tq, kc=kc, lambda_init=lambda_init)
            ys, w_out = [ya, yb], w_out_even[i]
        else:
            w = w_in_odd[i]
            w = jnp.concatenate([w[:, IN_ODD - C_W:], w[:, :IN_ODD - C_W]], axis=1).astype(BF16)
            if u is None:
                proj = _rms_inproj(h, norm_gain(layer), w, tm=_largest_tile(m, 1024), tn=IN_ODD // 4)
            else:
                proj = _matmul(u, w, tm=_largest_tile(m, 1024), tn=IN_ODD // 4)
            ys, w_out = [_dilated_attention(proj, b=b, s=s, kc=_largest_tile(s, 256))], w_out_odd[i]
        outs = _gate_outproj(ys, proj, w_out.astype(BF16), h, norm_gain(layer + 1),
                             tm=_largest_tile(m, 512), last=last)
        if last:
            return outs[0].reshape(b, s, d)
        h, u = outs
```

```python
import functools
import math

import jax
import jax.numpy as jnp
from jax import lax
from jax.experimental import pallas as pl
from jax.experimental.pallas import tpu as pltpu

F32 = jnp.float32
BF16 = jnp.bfloat16

HEAD_DIM = 128
A_HEADS = 8
A_KV_HEADS = 2
A_GROUP = A_HEADS // A_KV_HEADS
B_HEADS = 8
B_QK_DIM = 64
C_HEADS = 16
C_PATTERNS = ((128, 1), (512, 4), (2048, 16))
GRID_W = 64
ROPE_THETA = 10000.0
ROPE_PAIRS = HEAD_DIM // 4
NORM_EPS = 1e-6
SUBLN_EPS = 1e-5
NEG_INF = -1e30
LOG2E = math.log2(math.e)
QT = 256
ONES_ROWS = 16
SCORE_SLOTS = 3

A_Q = A_HEADS * HEAD_DIM
A_KV = A_KV_HEADS * HEAD_DIM
B_W = B_HEADS * HEAD_DIM
MIX = A_Q + B_W
EV_GATE, EV_QA, EV_KA, EV_VA = 0, MIX, MIX + A_Q, MIX + A_Q + A_KV
EV_QB = EV_VA + A_KV
EV_KB = EV_QB + B_W
EV_VB = EV_KB + B_W
IN_EVEN = EV_VB + B_W
C_W = C_HEADS * HEAD_DIM
OD_GATE, OD_Q, OD_K, OD_V = 0, C_W, 2 * C_W, 3 * C_W
IN_ODD = 4 * C_W

V7X_VMEM_LIMIT_BYTES = 56 * 1024 * 1024


def _params(n_axes):
    return pltpu.CompilerParams(dimension_semantics=("arbitrary",) * n_axes,
                                vmem_limit_bytes=V7X_VMEM_LIMIT_BYTES)


def _rms(x, eps):
    return x * lax.rsqrt(jnp.mean(x * x, axis=-1, keepdims=True) + eps)


def _rms_inproj_kernel(x_ref, g_ref, w_ref, o_ref, u_scr, *, strip):
    @pl.when(pl.program_id(1) == 0)
    def _():
        def body(r, c):
            sl = pl.ds(pl.multiple_of(r * strip, strip), strip)
            u_scr[sl, :] = (_rms(x_ref[sl, :], NORM_EPS) * g_ref[...]).astype(BF16)
            return c
        lax.fori_loop(0, x_ref.shape[0] // strip, body, 0)

    o_ref[...] = jnp.dot(u_scr[...], w_ref[...], preferred_element_type=F32).astype(o_ref.dtype)


def _rms_inproj(x2d, g, w, *, tm, tn):
    m, d = x2d.shape
    n = w.shape[1]
    strip = min(128, tm)
    return pl.pallas_call(
        functools.partial(_rms_inproj_kernel, strip=strip),
        grid=(m // tm, n // tn),
        in_specs=[pl.BlockSpec((tm, d), lambda i, j: (i, 0)),
                  pl.BlockSpec((1, d), lambda i, j: (0, 0)),
                  pl.BlockSpec((d, tn), lambda i, j: (0, j))],
        out_specs=pl.BlockSpec((tm, tn), lambda i, j: (i, j)),
        out_shape=jax.ShapeDtypeStruct((m, n), BF16),
        scratch_shapes=[pltpu.VMEM((tm, d), BF16)],
        compiler_params=_params(2),
        name="rms_inproj",
    )(x2d, g, w)


def _matmul_kernel(a_ref, w_ref, o_ref):
    o_ref[...] = jnp.dot(a_ref[...], w_ref[...], preferred_element_type=F32).astype(o_ref.dtype)


def _matmul(a, w, *, tm, tn):
    m, d = a.shape
    n = w.shape[1]
    return pl.pallas_call(
        _matmul_kernel,
        grid=(m // tm, n // tn),
        in_specs=[pl.BlockSpec((tm, d), lambda i, j: (i, 0)),
                  pl.BlockSpec((d, tn), lambda i, j: (0, j))],
        out_specs=pl.BlockSpec((tm, tn), lambda i, j: (i, j)),
        out_shape=jax.ShapeDtypeStruct((m, n), BF16),
        compiler_params=_params(2),
        name="inproj",
    )(a, w)


def _gate_outproj_kernel(*refs, n_y, last):
    y_refs = refs[:n_y]
    gate_ref, w_ref, res_ref, g_ref = refs[n_y:n_y + 4]
    outs = refs[n_y + 4:]
    y = jnp.concatenate([r[...].astype(F32) for r in y_refs], axis=-1) if n_y > 1 else y_refs[0][...].astype(F32)
    gate = gate_ref[...].astype(F32)
    y = (y * (gate / (1.0 + jnp.exp(-gate)))).astype(BF16)
    h = jnp.dot(y, w_ref[...], preferred_element_type=F32) + res_ref[...]
    u = _rms(h, NORM_EPS) * g_ref[...]
    if last:
        outs[0][...] = u.astype(outs[0].dtype)
    else:
        outs[0][...] = h
        outs[1][...] = u.astype(outs[1].dtype)


def _gate_outproj(ys, proj, w, res, g_next, *, tm, last):
    m, d = res.shape
    k = w.shape[0]
    in_specs = [pl.BlockSpec((tm, y.shape[1]), lambda i: (i, 0)) for y in ys]
    in_specs += [pl.BlockSpec((tm, k), lambda i: (i, 0)),
                 pl.BlockSpec((k, d), lambda i: (0, 0), pipeline_mode=pl.Buffered(1)),
                 pl.BlockSpec((tm, d), lambda i: (i, 0)),
                 pl.BlockSpec((1, d), lambda i: (0, 0))]
    if last:
        out_shape = [jax.ShapeDtypeStruct((m, d), F32)]
    else:
        out_shape = [jax.ShapeDtypeStruct((m, d), F32), jax.ShapeDtypeStruct((m, d), BF16)]
    out_specs = [pl.BlockSpec((tm, d), lambda i: (i, 0)) for _ in out_shape]
    return pl.pallas_call(
        functools.partial(_gate_outproj_kernel, n_y=len(ys), last=last),
        grid=(m // tm,),
        in_specs=in_specs,
        out_specs=out_specs,
        out_shape=out_shape,
        compiler_params=_params(1),
        name="gate_outproj",
    )(*ys, proj, w, res, g_next)


def _attend(n_tiles, q_tile, k_ref, vt_ref, s_scr, bias, finalize, chunks=None):
    ns, nk, kc, _ = s_scr.shape
    ahead = ns - 1
    sum_on_mxu = vt_ref.shape[0] > HEAD_DIM
    if chunks is None:
        chunks = lambda t: range(nk)
    col_max = {}

    def scores(t, c):
        s = lax.dot_general(k_ref[c * kc:(c + 1) * kc, :], q_tile(t), (((1,), (1,)), ((), ())),
                            preferred_element_type=F32)
        if bias is not None:
            s = s + bias(t, c)
        s_scr[t % ns, c] = s
        cm = jnp.max(s, axis=0, keepdims=True)
        col_max[t] = jnp.maximum(col_max[t], cm) if t in col_max else cm

    for t in range(min(ahead, n_tiles)):
        for c in chunks(t):
            scores(t, c)
    for t in range(n_tiles):
        m_cur = col_max.pop(t)
        l = acc = None
        later = list(chunks(t + ahead)) if t + ahead < n_tiles else []
        for i, c in enumerate(chunks(t)):
            if i < len(later):
                scores(t + ahead, later[i])
            x = s_scr[t % ns, c] - m_cur
            if sum_on_mxu:
                e = jnp.exp2(x.astype(BF16))
            else:
                e = jnp.exp2(x)
                ls = jnp.sum(e, axis=0, keepdims=True)
                l = ls if l is None else l + ls
                e = e.astype(BF16)
            pv = jnp.dot(vt_ref[:, c * kc:(c + 1) * kc], e, preferred_element_type=F32)
            acc = pv if acc is None else acc + pv
        for c in later[len(list(chunks(t))):]:
            scores(t + ahead, c)
        if sum_on_mxu:
            acc, l = acc[0:HEAD_DIM], acc[HEAD_DIM:HEAD_DIM + 1]
        finalize(t, acc, l)


def _transpose_into(vt_scr, v_ref):
    s_len = v_ref.shape[0]
    for r in range(s_len // QT):
        vt_scr[0:HEAD_DIM, r * QT:(r + 1) * QT] = v_ref[r * QT:(r + 1) * QT, :].astype(F32).T.astype(BF16)
    if vt_scr.shape[0] > HEAD_DIM:
        vt_scr[HEAD_DIM:, :] = jnp.ones((vt_scr.shape[0] - HEAD_DIM, s_len), BF16)


def _build_bias_table(t_scr, s_len, f):
    def body(r, carry):
        u = lax.broadcasted_iota(jnp.int32, (QT, QT), 0) + (r * QT - s_len)
        c = lax.broadcasted_iota(jnp.int32, (QT, QT), 1)
        t_scr[pl.ds(pl.multiple_of(r * QT, QT), QT), :] = f(u - c)
        return carry
    lax.fori_loop(0, t_scr.shape[0] // QT, body, 0)


def _norm_rope(x, g, cos, sin_up, sin_dn, scale):
    y = _rms(x.astype(F32), NORM_EPS) * g
    out = y * cos + pltpu.roll(y, 96, 1) * sin_up + pltpu.roll(y, 32, 1) * sin_dn
    if scale != 1.0:
        out = out * scale
    return out.astype(BF16)


def _rope_tables(s):
    pos = jnp.arange(s)
    row_ids = (pos // GRID_W).astype(F32)
    col_ids = (pos % GRID_W).astype(F32)
    inv_freq = ROPE_THETA ** (-jnp.arange(ROPE_PAIRS, dtype=F32) / ROPE_PAIRS)
    ang_r = row_ids[:, None] * inv_freq[None, :]
    ang_c = col_ids[:, None] * inv_freq[None, :]
    cr, sr, cc, sc = jnp.cos(ang_r), jnp.sin(ang_r), jnp.cos(ang_c), jnp.sin(ang_c)
    z = jnp.zeros_like(sr)
    cos = jnp.concatenate([cr, cr, cc, cc], axis=-1)
    sin_up = jnp.concatenate([-sr, z, -sc, z], axis=-1)
    sin_dn = jnp.concatenate([z, sr, z, sc], axis=-1)
    return cos, sin_up, sin_dn


def _slope(h, n_heads):
    hv = jnp.full((1, 1), h + 1, jnp.int32).astype(F32)
    return jnp.exp2(-8.0 * hv / n_heads)


def _gqa_kernel(q_ref, k_ref, v_ref, gq_ref, gk_ref, cos_ref, sup_ref, sdn_ref, o_ref,
                k_scr, vt_scr, q_scr, s_scr, *, strip):
    qi = pl.program_id(2)
    s_len = k_ref.shape[0]

    @pl.when(qi == 0)
    def _():
        def body(r, c):
            sl = pl.ds(pl.multiple_of(r * strip, strip), strip)
            k_scr[sl, :] = _norm_rope(k_ref[sl, :], gk_ref[...], cos_ref[sl, :], sup_ref[sl, :],
                                      sdn_ref[sl, :], 1.0)
            return c
        lax.fori_loop(0, s_len // strip, body, 0)
        _transpose_into(vt_scr, v_ref)

    tq = q_ref.shape[0]
    scale = LOG2E / math.sqrt(HEAD_DIM)
    for qb in range(tq // QT):
        rows = pl.ds(pl.multiple_of(qi * tq + qb * QT, QT), QT)
        cos, sup, sdn = cos_ref[rows, :], sup_ref[rows, :], sdn_ref[rows, :]
        for g in range(A_GROUP):
            t = qb * A_GROUP + g
            q_scr[t * QT:(t + 1) * QT, :] = _norm_rope(
                q_ref[qb * QT:(qb + 1) * QT, g * HEAD_DIM:(g + 1) * HEAD_DIM], gq_ref[...], cos, sup, sdn, scale)

    def finalize(t, acc, l):
        qb, g = divmod(t, A_GROUP)
        o_ref[qb * QT:(qb + 1) * QT, g * HEAD_DIM:(g + 1) * HEAD_DIM] = (acc * (1.0 / l)).T.astype(o_ref.dtype)

    _attend(A_GROUP * (tq // QT), lambda t: q_scr[t * QT:(t + 1) * QT, :], k_scr, vt_scr, s_scr, None, finalize)


def _gqa_attention(proj, gq, gk, tables, *, b, s, tq, kc):
    nq = s // tq
    gw = A_GROUP * HEAD_DIM
    full = lambda bi, kv, qi: (0, 0)
    return pl.pallas_call(
        functools.partial(_gqa_kernel, strip=min(256, s)),
        grid=(b, A_KV_HEADS, nq),
        in_specs=[pl.BlockSpec((tq, gw), lambda bi, kv, qi: (bi * nq + qi, EV_QA // gw + kv)),
                  pl.BlockSpec((s, HEAD_DIM), lambda bi, kv, qi: (bi, EV_KA // HEAD_DIM + kv)),
                  pl.BlockSpec((s, HEAD_DIM), lambda bi, kv, qi: (bi, EV_VA // HEAD_DIM + kv)),
                  pl.BlockSpec((1, HEAD_DIM), full),
                  pl.BlockSpec((1, HEAD_DIM), full),
                  pl.BlockSpec((s, HEAD_DIM), full),
                  pl.BlockSpec((s, HEAD_DIM), full),
                  pl.BlockSpec((s, HEAD_DIM), full)],
        out_specs=pl.BlockSpec((tq, gw), lambda bi, kv, qi: (bi * nq + qi, kv)),
        out_shape=jax.ShapeDtypeStruct((b * s, A_Q), BF16),
        scratch_shapes=[pltpu.VMEM((s, HEAD_DIM), BF16),
                        pltpu.VMEM((HEAD_DIM, s), BF16),
                        pltpu.VMEM((A_GROUP * tq, HEAD_DIM), BF16),
                        pltpu.VMEM((SCORE_SLOTS, s // kc, kc, QT), F32)],
        compiler_params=_params(3),
        name="gqa_attention",
    )(proj, proj, proj, gq, gk, *tables)


def _diff_kernel(q_ref, k_ref, v_ref, lq1_ref, lk1_ref, lq2_ref, lk2_ref, g_ref, o_ref,
                 q_scr, vt_scr, s_scr, t_scr, *, tq, lambda_init):
    h, b, qi = pl.program_id(0), pl.program_id(1), pl.program_id(2)
    s_len = k_ref.shape[0]
    kc = s_scr.shape[2]

    @pl.when(jnp.logical_and(b == 0, qi == 0))
    def _():
        slope = _slope(h, B_HEADS) * LOG2E
        _build_bias_table(t_scr, s_len, lambda d: -slope * jnp.abs(d.astype(F32)))

    @pl.when(qi == 0)
    def _():
        _transpose_into(vt_scr, v_ref)

    lane = lax.broadcasted_iota(jnp.int32, (tq, HEAD_DIM), 1)
    qf = q_ref[...].astype(F32) * (LOG2E / math.sqrt(B_QK_DIM))
    q_scr[0:tq, :] = jnp.where(lane < B_QK_DIM, qf, 0.0).astype(BF16)
    q_scr[tq:2 * tq, :] = jnp.where(lane < B_QK_DIM, 0.0, qf).astype(BF16)
    lam = (jnp.exp(jnp.sum(lq1_ref[...] * lk1_ref[...], axis=-1, keepdims=True))
           - jnp.exp(jnp.sum(lq2_ref[...] * lk2_ref[...], axis=-1, keepdims=True)) + lambda_init)

    def q_tile(t):
        r0 = (t % 2) * tq + (t // 2) * QT
        return q_scr[r0:r0 + QT, :]

    def bias(t, c):
        u0 = c * kc - (qi * tq + (t // 2) * QT) + s_len
        return t_scr[pl.ds(pl.multiple_of(u0, QT), kc), :]

    first_map = {}

    def finalize(t, acc, l):
        o = acc * (1.0 / l)
        if t % 2 == 0:
            first_map["o"] = o
            return
        o = first_map["o"] - lam * o
        o = o * lax.rsqrt(jnp.mean(o * o, axis=0, keepdims=True) + SUBLN_EPS) * g_ref[...] * (1.0 - lambda_init)
        o_ref[(t // 2) * QT:(t // 2 + 1) * QT, :] = o.T.astype(o_ref.dtype)

    _attend(2 * (tq // QT), q_tile, k_ref, vt_scr, s_scr, bias, finalize)


def _diff_attention(proj, lq1, lk1, lq2, lk2, g_col, *, b, s, tq, kc, lambda_init):
    nq = s // tq
    full = lambda h, bi, qi: (0, 0)
    return pl.pallas_call(
        functools.partial(_diff_kernel, tq=tq, lambda_init=lambda_init),
        grid=(B_HEADS, b, nq),
        in_specs=[pl.BlockSpec((tq, HEAD_DIM), lambda h, bi, qi: (bi * nq + qi, EV_QB // HEAD_DIM + h)),
                  pl.BlockSpec((s, HEAD_DIM), lambda h, bi, qi: (bi, EV_KB // HEAD_DIM + h)),
                  pl.BlockSpec((s, HEAD_DIM), lambda h, bi, qi: (bi, EV_VB // HEAD_DIM + h)),
                  pl.BlockSpec((1, B_QK_DIM), full),
                  pl.BlockSpec((1, B_QK_DIM), full),
                  pl.BlockSpec((1, B_QK_DIM), full),
                  pl.BlockSpec((1, B_QK_DIM), full),
                  pl.BlockSpec((HEAD_DIM, 1), full)],
        out_specs=pl.BlockSpec((tq, HEAD_DIM), lambda h, bi, qi: (bi * nq + qi, h)),
        out_shape=jax.ShapeDtypeStruct((b * s, B_W), BF16),
        scratch_shapes=[pltpu.VMEM((2 * tq, HEAD_DIM), BF16),
                        pltpu.VMEM((HEAD_DIM + ONES_ROWS, s), BF16),
                        pltpu.VMEM((SCORE_SLOTS, s // kc, kc, QT), F32),
                        pltpu.VMEM((2 * s, QT), F32)],
        compiler_params=_params(3),
        name="diff_attention",
    )(proj, proj, proj, lq1, lk1, lq2, lk2, g_col)


def _dilated_bias(d, slope):
    ad = jnp.abs(d.astype(F32))
    cnt = jnp.zeros(d.shape, F32)
    for window, dil in C_PATTERNS:
        cnt = cnt + jnp.where(((d & (dil - 1)) == 0) & (ad <= float(window // 2)), 1.0, 0.0)
    log_cnt = jnp.where(cnt > 2.5, math.log2(3.0), jnp.where(cnt > 1.5, 1.0, 0.0))
    return jnp.where(cnt > 0.5, log_cnt - slope * ad, NEG_INF)


def _dilated_kernel(q_ref, k_ref, v_ref, o_ref, q_scr, vt_scr, s_scr, t_scr):
    h, b = pl.program_id(0), pl.program_id(1)
    s_len = k_ref.shape[0]
    kc = s_scr.shape[2]
    reach = max(window // 2 for window, _ in C_PATTERNS)

    @pl.when(b == 0)
    def _():
        slope = _slope(h, C_HEADS) * LOG2E
        _build_bias_table(t_scr, s_len, lambda d: _dilated_bias(d, slope))

    _transpose_into(vt_scr, v_ref)
    q_scr[...] = (q_ref[...].astype(F32) * (LOG2E / math.sqrt(HEAD_DIM))).astype(BF16)

    def chunks(t):
        q_lo, q_hi = t * QT, (t + 1) * QT - 1
        return [c for c in range(s_len // kc)
                if c * kc - q_hi <= reach and q_lo - ((c + 1) * kc - 1) <= reach]

    def bias(t, c):
        u0 = c * kc - t * QT + s_len
        return t_scr[u0:u0 + kc, :]

    def finalize(t, acc, l):
        o_ref[t * QT:(t + 1) * QT, :] = (acc * (1.0 / l)).T.astype(o_ref.dtype)

    _attend(s_len // QT, lambda t: q_scr[t * QT:(t + 1) * QT, :], k_ref, vt_scr, s_scr, bias, finalize, chunks)


def _dilated_attention(proj, *, b, s, kc):
    return pl.pallas_call(
        _dilated_kernel,
        grid=(C_HEADS, b),
        in_specs=[pl.BlockSpec((s, HEAD_DIM), lambda h, bi: (bi, OD_Q // HEAD_DIM + h)),
                  pl.BlockSpec((s, HEAD_DIM), lambda h, bi: (bi, OD_K // HEAD_DIM + h)),
                  pl.BlockSpec((s, HEAD_DIM), lambda h, bi: (bi, OD_V // HEAD_DIM + h))],
        out_specs=pl.BlockSpec((s, HEAD_DIM), lambda h, bi: (bi, h)),
        out_shape=jax.ShapeDtypeStruct((b * s, C_W), BF16),
        scratch_shapes=[pltpu.VMEM((s, HEAD_DIM), BF16),
                        pltpu.VMEM((HEAD_DIM + ONES_ROWS, s), BF16),
                        pltpu.VMEM((SCORE_SLOTS, s // kc, kc, QT), F32),
                        pltpu.VMEM((2 * s, QT), F32)],
        compiler_params=_params(2),
        name="dilated_attention",
    )(proj, proj, proj)


def _largest_tile(n, cap):
    t = min(n, cap)
    while n % t:
        t //= 2
    return t


def kernel(x, ln_even_g, w_in_even, a_q_norm_g, a_k_norm_g, b_lambda_q1, b_lambda_k1, b_lambda_q2,
           b_lambda_k2, b_subln_g, w_out_even, ln_odd_g, w_in_odd, w_out_odd, final_norm_g):
    b, s, d = x.shape
    m = b * s
    depth = ln_even_g.shape[0] + ln_odd_g.shape[0]
    assert s % GRID_W == 0 and s % QT == 0 and d == MIX
    kc = _largest_tile(s, 512)
    tables = _rope_tables(s)
    row = lambda v: v.reshape(1, -1).astype(F32)

    def norm_gain(layer):
        if layer == depth:
            return row(final_norm_g)
        return row(ln_even_g[layer // 2] if layer % 2 == 0 else ln_odd_g[layer // 2])

    h = x.reshape(m, d)
    u = None
    for layer in range(depth):
        i = layer // 2
        last = layer == depth - 1
        if layer % 2 == 0:
            w = w_in_even[i]
            w = jnp.concatenate([w[:, IN_EVEN - MIX:], w[:, :IN_EVEN - MIX]], axis=1).astype(BF16)
            if u is None:
                proj = _rms_inproj(h, norm_gain(layer), w, tm=_largest_tile(m, 512), tn=IN_EVEN // 2)
            else:
                proj = _matmul(u, w, tm=_largest_tile(m, 512), tn=IN_EVEN // 2)
            ya = _gqa_attention(proj, row(a_q_norm_g[i]), row(a_k_norm_g[i]), tables, b=b, s=s,
                                tq=_largest_tile(s, 2 * QT), kc=kc)
            lambda_init = 0.8 - 0.6 * math.exp(-0.3 * layer)
            yb = _diff_attention(proj, row(b_lambda_q1[i]), row(b_lambda_k1[i]), row(b_lambda_q2[i]),
                                 row(b_lambda_k2[i]), b_subln_g[i].reshape(-1, 1).astype(F32),
                                 b=b, s=s, tq=s, kc=kc, lambda_init=lambda_init)
            ys, w_out = [ya, yb], w_out_even[i]
        else:
            w = w_in_odd[i]
            w = jnp.concatenate([w[:, IN_ODD - C_W:], w[:, :IN_ODD - C_W]], axis=1).astype(BF16)
            if u is None:
                proj = _rms_inproj(h, norm_gain(layer), w, tm=_largest_tile(m, 1024), tn=IN_ODD // 4)
            else:
                proj = _matmul(u, w, tm=_largest_tile(m, 1024), tn=IN_ODD // 4)
            ys, w_out = [_dilated_attention(proj, b=b, s=s, kc=_largest_tile(s, 256))], w_out_odd[i]
        outs = _gate_outproj(ys, proj, w_out.astype(BF16), h, norm_gain(layer + 1),
                             tm=_largest_tile(m, 512), last=last)
        if last:
            return outs[0].reshape(b, s, d)
        h, u = outs
```

```python
import functools
import math

import jax
import jax.numpy as jnp
from jax import lax
from jax.experimental import pallas as pl
from jax.experimental.pallas import tpu as pltpu

F32 = jnp.float32
BF16 = jnp.bfloat16

HEAD_DIM = 128
A_HEADS = 8
A_KV_HEADS = 2
A_GROUP = A_HEADS // A_KV_HEADS
B_HEADS = 8
B_QK_DIM = 64
C_HEADS = 16
C_PATTERNS = ((128, 1), (512, 4), (2048, 16))
GRID_W = 64
ROPE_THETA = 10000.0
ROPE_PAIRS = HEAD_DIM // 4
NORM_EPS = 1e-6
SUBLN_EPS = 1e-5
NEG_INF = -1e30
LOG2E = math.log2(math.e)
QT = 256
ONES_ROWS = 16
SCORE_SLOTS = 3

A_Q = A_HEADS * HEAD_DIM
A_KV = A_KV_HEADS * HEAD_DIM
B_W = B_HEADS * HEAD_DIM
MIX = A_Q + B_W
EV_QA, EV_KA, EV_VA = 0, A_Q, A_Q + A_KV
EV_QB = EV_VA + A_KV
EV_KB = EV_QB + B_W
EV_VB = EV_KB + B_W
EV_GATE = EV_VB + B_W
IN_EVEN = EV_GATE + MIX
C_W = C_HEADS * HEAD_DIM
OD_Q, OD_K, OD_V, OD_GATE = 0, C_W, 2 * C_W, 3 * C_W
IN_ODD = 4 * C_W

V7X_VMEM_LIMIT_BYTES = 56 * 1024 * 1024


def _params(n_axes):
    return pltpu.CompilerParams(dimension_semantics=("arbitrary",) * n_axes,
                                vmem_limit_bytes=V7X_VMEM_LIMIT_BYTES)


def _rms(x, eps):
    return x * lax.rsqrt(jnp.mean(x * x, axis=-1, keepdims=True) + eps)


def _rms_inproj_kernel(x_ref, g_ref, w_ref, o_ref, u_scr, *, strip):
    @pl.when(pl.program_id(1) == 0)
    def _():
        def body(r, c):
            sl = pl.ds(pl.multiple_of(r * strip, strip), strip)
            u_scr[sl, :] = (_rms(x_ref[sl, :], NORM_EPS) * g_ref[...]).astype(BF16)
            return c
        lax.fori_loop(0, x_ref.shape[0] // strip, body, 0)

    o_ref[...] = jnp.dot(u_scr[...], w_ref[...], preferred_element_type=F32).astype(o_ref.dtype)


def _rms_inproj(x2d, g, w, *, tm, tn):
    m, d = x2d.shape
    n = w.shape[1]
    strip = min(128, tm)
    return pl.pallas_call(
        functools.partial(_rms_inproj_kernel, strip=strip),
        grid=(m // tm, n // tn),
        in_specs=[pl.BlockSpec((tm, d), lambda i, j: (i, 0)),
                  pl.BlockSpec((1, d), lambda i, j: (0, 0)),
                  pl.BlockSpec((d, tn), lambda i, j: (0, j))],
        out_specs=pl.BlockSpec((tm, tn), lambda i, j: (i, j)),
        out_shape=jax.ShapeDtypeStruct((m, n), BF16),
        scratch_shapes=[pltpu.VMEM((tm, d), BF16)],
        compiler_params=_params(2),
        name="rms_inproj",
    )(x2d, g, w)


def _matmul_kernel(a_ref, w_ref, o_ref):
    o_ref[...] = jnp.dot(a_ref[...], w_ref[...], preferred_element_type=F32).astype(o_ref.dtype)


def _matmul(a, w, *, tm, tn):
    m, d = a.shape
    n = w.shape[1]
    return pl.pallas_call(
        _matmul_kernel,
        grid=(m // tm, n // tn),
        in_specs=[pl.BlockSpec((tm, d), lambda i, j: (i, 0)),
                  pl.BlockSpec((d, tn), lambda i, j: (0, j))],
        out_specs=pl.BlockSpec((tm, tn), lambda i, j: (i, j)),
        out_shape=jax.ShapeDtypeStruct((m, n), BF16),
        compiler_params=_params(2),
        name="inproj",
    )(a, w)


def _cat(refs):
    vals = [r[...].astype(F32) for r in refs]
    return jnp.concatenate(vals, axis=-1) if len(vals) > 1 else vals[0]


def _gate_outproj_kernel(*refs, n_y, n_g, last):
    y_refs, gate_refs = refs[:n_y], refs[n_y:n_y + n_g]
    w_ref, res_ref, g_ref = refs[n_y + n_g:n_y + n_g + 3]
    outs = refs[n_y + n_g + 3:]
    y = _cat(y_refs)
    gate = _cat(gate_refs)
    y = (y * (gate / (1.0 + jnp.exp(-gate)))).astype(BF16)
    h = jnp.dot(y, w_ref[...], preferred_element_type=F32) + res_ref[...]
    u = _rms(h, NORM_EPS) * g_ref[...]
    if last:
        outs[0][...] = u.astype(outs[0].dtype)
    else:
        outs[0][...] = h
        outs[1][...] = u.astype(outs[1].dtype)


def _gate_outproj(ys, proj, gate_off, w, res, g_next, *, tm, last):
    m, d = res.shape
    k = w.shape[0]
    gw = math.gcd(gate_off, k)
    n_g = k // gw
    in_specs = [pl.BlockSpec((tm, y.shape[1]), lambda i: (i, 0)) for y in ys]
    in_specs += [pl.BlockSpec((tm, gw), functools.partial(lambda i, j: (i, gate_off // gw + j), j=j))
                 for j in range(n_g)]
    in_specs += [pl.BlockSpec((k, d), lambda i: (0, 0), pipeline_mode=pl.Buffered(1)),
                 pl.BlockSpec((tm, d), lambda i: (i, 0)),
                 pl.BlockSpec((1, d), lambda i: (0, 0))]
    if last:
        out_shape = [jax.ShapeDtypeStruct((m, d), F32)]
    else:
        out_shape = [jax.ShapeDtypeStruct((m, d), F32), jax.ShapeDtypeStruct((m, d), BF16)]
    out_specs = [pl.BlockSpec((tm, d), lambda i: (i, 0)) for _ in out_shape]
    return pl.pallas_call(
        functools.partial(_gate_outproj_kernel, n_y=len(ys), n_g=n_g, last=last),
        grid=(m // tm,),
        in_specs=in_specs,
        out_specs=out_specs,
        out_shape=out_shape,
        compiler_params=_params(1),
        name="gate_outproj",
    )(*ys, *([proj] * n_g), w, res, g_next)


def _attend(n_tiles, q_tile, k_ref, vt_ref, s_scr, bias, finalize, chunks=None):
    ns, nk, kc, _ = s_scr.shape
    ahead = ns - 1
    sum_on_mxu = vt_ref.shape[0] > HEAD_DIM
    if chunks is None:
        chunks = lambda t: range(nk)
    col_max = {}

    def scores(t, c):
        s = lax.dot_general(k_ref[c * kc:(c + 1) * kc, :], q_tile(t), (((1,), (1,)), ((), ())),
                            preferred_element_type=F32)
        if bias is not None:
            s = s + bias(t, c)
        s_scr[t % ns, c] = s
        cm = jnp.max(s, axis=0, keepdims=True)
        col_max[t] = jnp.maximum(col_max[t], cm) if t in col_max else cm

    for t in range(min(ahead, n_tiles)):
        for c in chunks(t):
            scores(t, c)
    for t in range(n_tiles):
        m_cur = col_max.pop(t)
        l = acc = None
        later = list(chunks(t + ahead)) if t + ahead < n_tiles else []
        for i, c in enumerate(chunks(t)):
            if i < len(later):
                scores(t + ahead, later[i])
            x = s_scr[t % ns, c] - m_cur
            if sum_on_mxu:
                e = jnp.exp2(x.astype(BF16))
            else:
                e = jnp.exp2(x)
                ls = jnp.sum(e, axis=0, keepdims=True)
                l = ls if l is None else l + ls
                e = e.astype(BF16)
            pv = jnp.dot(vt_ref[:, c * kc:(c + 1) * kc], e, preferred_element_type=F32)
            acc = pv if acc is None else acc + pv
        for c in later[len(list(chunks(t))):]:
            scores(t + ahead, c)
        if sum_on_mxu:
            acc, l = acc[0:HEAD_DIM], acc[HEAD_DIM:HEAD_DIM + 1]
        finalize(t, acc, l)


def _transpose_into(vt_scr, v_ref):
    s_len = v_ref.shape[0]
    for r in range(s_len // QT):
        vt_scr[0:HEAD_DIM, r * QT:(r + 1) * QT] = v_ref[r * QT:(r + 1) * QT, :].astype(F32).T.astype(BF16)
    if vt_scr.shape[0] > HEAD_DIM:
        vt_scr[HEAD_DIM:, :] = jnp.ones((vt_scr.shape[0] - HEAD_DIM, s_len), BF16)


def _build_bias_table(t_scr, s_len, f):
    def body(r, carry):
        u = lax.broadcasted_iota(jnp.int32, (QT, QT), 0) + (r * QT - s_len)
        c = lax.broadcasted_iota(jnp.int32, (QT, QT), 1)
        t_scr[pl.ds(pl.multiple_of(r * QT, QT), QT), :] = f(u - c)
        return carry
    lax.fori_loop(0, t_scr.shape[0] // QT, body, 0)


def _norm_rope(x, g, cos, sin_up, sin_dn, scale):
    y = _rms(x.astype(F32), NORM_EPS) * g
    out = y * cos + pltpu.roll(y, 96, 1) * sin_up + pltpu.roll(y, 32, 1) * sin_dn
    if scale != 1.0:
        out = out * scale
    return out.astype(BF16)


def _rope_tables(s):
    pos = jnp.arange(s)
    row_ids = (pos // GRID_W).astype(F32)
    col_ids = (pos % GRID_W).astype(F32)
    inv_freq = ROPE_THETA ** (-jnp.arange(ROPE_PAIRS, dtype=F32) / ROPE_PAIRS)
    ang_r = row_ids[:, None] * inv_freq[None, :]
    ang_c = col_ids[:, None] * inv_freq[None, :]
    cr, sr, cc, sc = jnp.cos(ang_r), jnp.sin(ang_r), jnp.cos(ang_c), jnp.sin(ang_c)
    z = jnp.zeros_like(sr)
    cos = jnp.concatenate([cr, cr, cc, cc], axis=-1)
    sin_up = jnp.concatenate([-sr, z, -sc, z], axis=-1)
    sin_dn = jnp.concatenate([z, sr, z, sc], axis=-1)
    return cos, sin_up, sin_dn


def _slope(h, n_heads):
    hv = jnp.full((1, 1), h + 1, jnp.int32).astype(F32)
    return jnp.exp2(-8.0 * hv / n_heads)


def _gqa_kernel(q_ref, k_ref, v_ref, gq_ref, gk_ref, cos_ref, sup_ref, sdn_ref, o_ref,
                k_scr, vt_scr, q_scr, s_scr, *, strip):
    qi = pl.program_id(2)
    s_len = k_ref.shape[0]

    @pl.when(qi == 0)
    def _():
        def body(r, c):
            sl = pl.ds(pl.multiple_of(r * strip, strip), strip)
            k_scr[sl, :] = _norm_rope(k_ref[sl, :], gk_ref[...], cos_ref[sl, :], sup_ref[sl, :],
                                      sdn_ref[sl, :], 1.0)
            return c
        lax.fori_loop(0, s_len // strip, body, 0)
        _transpose_into(vt_scr, v_ref)

    tq = q_ref.shape[0]
    scale = LOG2E / math.sqrt(HEAD_DIM)
    for qb in range(tq // QT):
        rows = pl.ds(pl.multiple_of(qi * tq + qb * QT, QT), QT)
        cos, sup, sdn = cos_ref[rows, :], sup_ref[rows, :], sdn_ref[rows, :]
        for g in range(A_GROUP):
            t = qb * A_GROUP + g
            q_scr[t * QT:(t + 1) * QT, :] = _norm_rope(
                q_ref[qb * QT:(qb + 1) * QT, g * HEAD_DIM:(g + 1) * HEAD_DIM], gq_ref[...], cos, sup, sdn, scale)

    def finalize(t, acc, l):
        qb, g = divmod(t, A_GROUP)
        o_ref[qb * QT:(qb + 1) * QT, g * HEAD_DIM:(g + 1) * HEAD_DIM] = (acc * (1.0 / l)).T.astype(o_ref.dtype)

    _attend(A_GROUP * (tq // QT), lambda t: q_scr[t * QT:(t + 1) * QT, :], k_scr, vt_scr, s_scr, None, finalize)


def _gqa_attention(proj, gq, gk, tables, *, b, s, tq, kc):
    nq = s // tq
    gw = A_GROUP * HEAD_DIM
    full = lambda bi, kv, qi: (0, 0)
    return pl.pallas_call(
        functools.partial(_gqa_kernel, strip=min(256, s)),
        grid=(b, A_KV_HEADS, nq),
        in_specs=[pl.BlockSpec((tq, gw), lambda bi, kv, qi: (bi * nq + qi, EV_QA // gw + kv)),
                  pl.BlockSpec((s, HEAD_DIM), lambda bi, kv, qi: (bi, EV_KA // HEAD_DIM + kv)),
                  pl.BlockSpec((s, HEAD_DIM), lambda bi, kv, qi: (bi, EV_VA // HEAD_DIM + kv)),
                  pl.BlockSpec((1, HEAD_DIM), full),
                  pl.BlockSpec((1, HEAD_DIM), full),
                  pl.BlockSpec((s, HEAD_DIM), full),
                  pl.BlockSpec((s, HEAD_DIM), full),
                  pl.BlockSpec((s, HEAD_DIM), full)],
        out_specs=pl.BlockSpec((tq, gw), lambda bi, kv, qi: (bi * nq + qi, kv)),
        out_shape=jax.ShapeDtypeStruct((b * s, A_Q), BF16),
        scratch_shapes=[pltpu.VMEM((s, HEAD_DIM), BF16),
                        pltpu.VMEM((HEAD_DIM, s), BF16),
                        pltpu.VMEM((A_GROUP * tq, HEAD_DIM), BF16),
                        pltpu.VMEM((SCORE_SLOTS, s // kc, kc, QT), F32)],
        compiler_params=_params(3),
        name="gqa_attention",
    )(proj, proj, proj, gq, gk, *tables)


def _diff_kernel(q_ref, k_ref, v_ref, lq1_ref, lk1_ref, lq2_ref, lk2_ref, g_ref, o_ref,
                 q_scr, vt_scr, s_scr, t_scr, *, lambda_init):
    h, b = pl.program_id(0), pl.program_id(1)
    tq = s_len = k_ref.shape[0]
    kc = s_scr.shape[2]

    @pl.when(b == 0)
    def _():
        slope = _slope(h, B_HEADS) * LOG2E
        _build_bias_table(t_scr, s_len, lambda d: -slope * jnp.abs(d.astype(F32)))

    _transpose_into(vt_scr, v_ref)

    lane = lax.broadcasted_iota(jnp.int32, (tq, HEAD_DIM), 1)
    qf = q_ref[...].astype(F32) * (LOG2E / math.sqrt(B_QK_DIM))
    q_scr[0:tq, :] = jnp.where(lane < B_QK_DIM, qf, 0.0).astype(BF16)
    q_scr[tq:2 * tq, :] = jnp.where(lane < B_QK_DIM, 0.0, qf).astype(BF16)
    lam = (jnp.exp(jnp.sum(lq1_ref[...] * lk1_ref[...], axis=-1, keepdims=True))
           - jnp.exp(jnp.sum(lq2_ref[...] * lk2_ref[...], axis=-1, keepdims=True)) + lambda_init)

    def q_tile(t):
        r0 = (t % 2) * tq + (t // 2) * QT
        return q_scr[r0:r0 + QT, :]

    def bias(t, c):
        u0 = c * kc - (t // 2) * QT + s_len
        return t_scr[u0:u0 + kc, :]

    first_map = {}

    def finalize(t, acc, l):
        o = acc * (1.0 / l)
        if t % 2 == 0:
            first_map["o"] = o
            return
        o = first_map["o"] - lam * o
        o = o * lax.rsqrt(jnp.mean(o * o, axis=0, keepdims=True) + SUBLN_EPS) * g_ref[...] * (1.0 - lambda_init)
        o_ref[(t // 2) * QT:(t // 2 + 1) * QT, :] = o.T.astype(o_ref.dtype)

    _attend(2 * (tq // QT), q_tile, k_ref, vt_scr, s_scr, bias, finalize)


def _diff_attention(proj, lq1, lk1, lq2, lk2, g_col, *, b, s, kc, lambda_init):
    full = lambda h, bi: (0, 0)
    return pl.pallas_call(
        functools.partial(_diff_kernel, lambda_init=lambda_init),
        grid=(B_HEADS, b),
        in_specs=[pl.BlockSpec((s, HEAD_DIM), lambda h, bi: (bi, EV_QB // HEAD_DIM + h)),
                  pl.BlockSpec((s, HEAD_DIM), lambda h, bi: (bi, EV_KB // HEAD_DIM + h)),
                  pl.BlockSpec((s, HEAD_DIM), lambda h, bi: (bi, EV_VB // HEAD_DIM + h)),
                  pl.BlockSpec((1, B_QK_DIM), full),
                  pl.BlockSpec((1, B_QK_DIM), full),
                  pl.BlockSpec((1, B_QK_DIM), full),
                  pl.BlockSpec((1, B_QK_DIM), full),
                  pl.BlockSpec((HEAD_DIM, 1), full)],
        out_specs=pl.BlockSpec((s, HEAD_DIM), lambda h, bi: (bi, h)),
        out_shape=jax.ShapeDtypeStruct((b * s, B_W), BF16),
        scratch_shapes=[pltpu.VMEM((2 * s, HEAD_DIM), BF16),
                        pltpu.VMEM((HEAD_DIM + ONES_ROWS, s), BF16),
                        pltpu.VMEM((SCORE_SLOTS, s // kc, kc, QT), F32),
                        pltpu.VMEM((2 * s, QT), F32)],
        compiler_params=_params(2),
        name="diff_attention",
    )(proj, proj, proj, lq1, lk1, lq2, lk2, g_col)


def _dilated_bias(d, slope):
    ad = jnp.abs(d.astype(F32))
    cnt = jnp.zeros(d.shape, F32)
    for window, dil in C_PATTERNS:
        cnt = cnt + jnp.where(((d & (dil - 1)) == 0) & (ad <= float(window // 2)), 1.0, 0.0)
    log_cnt = jnp.where(cnt > 2.5, math.log2(3.0), jnp.where(cnt > 1.5, 1.0, 0.0))
    return jnp.where(cnt > 0.5, log_cnt - slope * ad, NEG_INF)


def _dilated_kernel(q_ref, k_ref, v_ref, o_ref, q_scr, vt_scr, s_scr, t_scr):
    h, b = pl.program_id(0), pl.program_id(1)
    s_len = k_ref.shape[0]
    kc = s_scr.shape[2]
    reach = max(window // 2 for window, _ in C_PATTERNS)

    @pl.when(b == 0)
    def _():
        slope = _slope(h, C_HEADS) * LOG2E
        _build_bias_table(t_scr, s_len, lambda d: _dilated_bias(d, slope))

    _transpose_into(vt_scr, v_ref)
    q_scr[...] = (q_ref[...].astype(F32) * (LOG2E / math.sqrt(HEAD_DIM))).astype(BF16)

    def chunks(t):
        q_lo, q_hi = t * QT, (t + 1) * QT - 1
        return [c for c in range(s_len // kc)
                if c * kc - q_hi <= reach and q_lo - ((c + 1) * kc - 1) <= reach]

    def bias(t, c):
        u0 = c * kc - t * QT + s_len
        return t_scr[u0:u0 + kc, :]

    def finalize(t, acc, l):
        o_ref[t * QT:(t + 1) * QT, :] = (acc * (1.0 / l)).T.astype(o_ref.dtype)

    _attend(s_len // QT, lambda t: q_scr[t * QT:(t + 1) * QT, :], k_ref, vt_scr, s_scr, bias, finalize, chunks)


def _dilated_attention(proj, *, b, s, kc):
    return pl.pallas_call(
        _dilated_kernel,
        grid=(C_HEADS, b),
        in_specs=[pl.BlockSpec((s, HEAD_DIM), lambda h, bi: (bi, OD_Q // HEAD_DIM + h)),
                  pl.BlockSpec((s, HEAD_DIM), lambda h, bi: (bi, OD_K // HEAD_DIM + h)),
                  pl.BlockSpec((s, HEAD_DIM), lambda h, bi: (bi, OD_V // HEAD_DIM + h))],
        out_specs=pl.BlockSpec((s, HEAD_DIM), lambda h, bi: (bi, h)),
        out_shape=jax.ShapeDtypeStruct((b * s, C_W), BF16),
        scratch_shapes=[pltpu.VMEM((s, HEAD_DIM), BF16),
                        pltpu.VMEM((HEAD_DIM + ONES_ROWS, s), BF16),
                        pltpu.VMEM((SCORE_SLOTS, s // kc, kc, QT), F32),
                        pltpu.VMEM((2 * s, QT), F32)],
        compiler_params=_params(2),
        name="dilated_attention",
    )(proj, proj, proj)


def _largest_tile(n, cap):
    t = min(n, cap)
    while n % t:
        t //= 2
    return t


def kernel(x, ln_even_g, w_in_even, a_q_norm_g, a_k_norm_g, b_lambda_q1, b_lambda_k1, b_lambda_q2,
           b_lambda_k2, b_subln_g, w_out_even, ln_odd_g, w_in_odd, w_out_odd, final_norm_g):
    b, s, d = x.shape
    m = b * s
    depth = ln_even_g.shape[0] + ln_odd_g.shape[0]
    assert s % GRID_W == 0 and s % QT == 0 and d == MIX
    kc = _largest_tile(s, 512)
    tables = _rope_tables(s)
    row = lambda v: v.reshape(1, -1).astype(F32)

    def norm_gain(layer):
        if layer == depth:
            return row(final_norm_g)
        return row(ln_even_g[layer // 2] if layer % 2 == 0 else ln_odd_g[layer // 2])

    h = x.reshape(m, d)
    u = None
    for layer in range(depth):
        i = layer // 2
        last = layer == depth - 1
        if layer % 2 == 0:
            w = w_in_even[i].astype(BF16)
            if u is None:
                proj = _rms_inproj(h, norm_gain(layer), w, tm=_largest_tile(m, 512), tn=IN_EVEN // 2)
            else:
                proj = _matmul(u, w, tm=_largest_tile(m, 512), tn=IN_EVEN // 2)
            ya = _gqa_attention(proj, row(a_q_norm_g[i]), row(a_k_norm_g[i]), tables, b=b, s=s,
                                tq=_largest_tile(s, 4 * QT), kc=kc)
            lambda_init = 0.8 - 0.6 * math.exp(-0.3 * layer)
            yb = _diff_attention(proj, row(b_lambda_q1[i]), row(b_lambda_k1[i]), row(b_lambda_q2[i]),
                                 row(b_lambda_k2[i]), b_subln_g[i].reshape(-1, 1).astype(F32),
                                 b=b, s=s, kc=kc, lambda_init=lambda_init)
            ys, w_out, gate_off = [ya, yb], w_out_even[i], EV_GATE
        else:
            w = w_in_odd[i].astype(BF16)
            if u is None:
                proj = _rms_inproj(h, norm_gain(layer), w, tm=_largest_tile(m, 1024), tn=IN_ODD // 4)
            else:
                proj = _matmul(u, w, tm=_largest_tile(m, 1024), tn=IN_ODD // 4)
            ys, w_out = [_dilated_attention(proj, b=b, s=s, kc=_largest_tile(s, 256))], w_out_odd[i]
            gate_off = OD_GATE
        outs = _gate_outproj(ys, proj, gate_off, w_out.astype(BF16), h, norm_gain(layer + 1),
                             tm=_largest_tile(m, 512), last=last)
        if last:
            return outs[0].reshape(b, s, d)
        h, u = outs
```
